```python
import jax, jax.numpy as jnp
from jax import lax
import numpy as np

D_MODEL = 2048
BATCH = 4
SEQ = 2048
DEPTH = 1
DEC_BATCH = 128
DEC_SEQ = 8
PAST_LEN = 16384
PAGE_SIZE = 128

H_A = 8
KV_A = 2
DH_A = 128
H_IDX = 16
D_IDX = 64
TOPK_MAX = 256
H_B = 8
Q_LORA = 512
KV_LORA = 256
DN_B = 128
DR_B = 64
DV_B = 128
D_FF = 5632
CONV_W = 3
ROPE_THETA = 10000.0
EPS = 1e-6
QBLK = 128
MIX_W = H_A * DH_A + H_B * DV_B
IN_SIZES = (H_A * DH_A, KV_A * DH_A, KV_A * DH_A, H_IDX * D_IDX, D_IDX, H_IDX, Q_LORA, KV_LORA, DR_B)
IN_COLS = sum(IN_SIZES)
IN_OFFSETS = tuple(sum(IN_SIZES[:i + 1]) for i in range(len(IN_SIZES) - 1))

kernel_name = 'hymba_dsa_mla_convffn_step'


def rms_norm(x, g):
    xf = x.astype(jnp.float32)
    y = xf * lax.rsqrt(jnp.mean(xf * xf, axis=-1, keepdims=True) + EPS)
    return (y * g.astype(jnp.float32)).astype(x.dtype)


def rope(x, pos):
    half = x.shape[-1] // 2
    inv = jnp.power(ROPE_THETA, -jnp.arange(half, dtype=jnp.float32) / half)
    ang = pos.astype(jnp.float32)[:, None] * inv[None, :]
    ang = ang.reshape(ang.shape[:1] + (1,) * (x.ndim - 3) + (half,))
    cos, sin = jnp.cos(ang), jnp.sin(ang)
    xf = x.astype(jnp.float32)
    x1, x2 = xf[..., :half], xf[..., half:]
    return jnp.concatenate([x1 * cos - x2 * sin, x2 * cos + x1 * sin], axis=-1).astype(x.dtype)


def rows_at(rows, idx):
    return jax.vmap(lambda r, i: r[i])(rows, idx)


def project_heads(h, pos, w_in, g_q, w_q_b, g_kv, w_uk):
    b, t, _ = h.shape
    z = jnp.einsum('btd,de->bte', h, w_in)
    qa, ka, va, iq, ik, iw, cq, ckv, kr = jnp.split(z, IN_OFFSETS, axis=-1)
    qa = rope(qa.reshape(b, t, H_A, DH_A), pos)
    ka = rope(ka.reshape(b, t, KV_A, DH_A), pos)
    va = va.reshape(b, t, KV_A, DH_A)
    iq = rope(iq.reshape(b, t, H_IDX, D_IDX), pos)
    ik = rope(ik, pos)
    iw = iw * (H_IDX ** -0.5)
    qb = jnp.einsum('btr,rhe->bthe', rms_norm(cq, g_q), w_q_b)
    q_nope = qb[..., :DN_B]
    q_rope = rope(qb[..., DN_B:], pos)
    q_lat = jnp.einsum('bthn,chn->bthc', q_nope, w_uk)
    ckv = rms_norm(ckv, g_kv)
    kr = rope(kr, pos)
    return (qa, iq, iw, q_lat, q_rope), (ka, va, ik, ckv, kr)


def dsa_select(iq, iw, qpos, ik_all, topk):
    dots = jnp.einsum('bqhd,bsd->bqhs', iq, ik_all) * (D_IDX ** -0.5)
    score = jnp.einsum('bqhs,bqh->bqs', jax.nn.relu(dots), iw).astype(jnp.float32)
    kpos = jnp.arange(ik_all.shape[1], dtype=jnp.int32)
    score = jnp.where(kpos[None, None, :] <= qpos[None, :, None], score, -jnp.inf)
    _, idx = lax.top_k(score, topk)
    valid = idx <= qpos[None, :, None]
    return idx, valid


def sparse_attend(qa, k_sel, v_sel, valid):
    b, q = qa.shape[:2]
    qg = qa.reshape(b, q, KV_A, H_A // KV_A, DH_A)
    logits = jnp.einsum('bqkgd,bqskd->bqkgs', qg, k_sel).astype(jnp.float32) * (DH_A ** -0.5)
    logits = jnp.where(valid[:, :, None, None, :], logits, -jnp.inf)
    p = jax.nn.softmax(logits, axis=-1).astype(v_sel.dtype)
    o = jnp.einsum('bqkgs,bqskd->bqkgd', p, v_sel)
    return o.reshape(b, q, H_A * DH_A)


def mla_attend(q_lat, q_rope, qpos, ckv, kr, w_uv):
    logits = (jnp.einsum('bqhc,bsc->bhqs', q_lat, ckv)
              + jnp.einsum('bqhr,bsr->bhqs', q_rope, kr)).astype(jnp.float32) * ((DN_B + DR_B) ** -0.5)
    kpos = jnp.arange(ckv.shape[1], dtype=jnp.int32)
    logits = jnp.where(kpos[None, None, None, :] <= qpos[None, None, :, None], logits, -jnp.inf)
    p = jax.nn.softmax(logits, axis=-1).astype(ckv.dtype)
    o_lat = jnp.einsum('bhqs,bsc->bqhc', p, ckv)
    o = jnp.einsum('bqhc,chv->bqhv', o_lat, w_uv)
    return o.reshape(o.shape[0], o.shape[1], H_B * DV_B)


def prompt_mix(qs, ks, w_uv):
    qa, iq, iw, q_lat, q_rope = qs
    ka, va, ik, ckv, kr = ks
    b, t = qa.shape[:2]
    nb = t // QBLK
    topk = min(TOPK_MAX, t // 4)

    def block(args):
        qa_b, iq_b, iw_b, ql_b, qr_b, qpos = args
        idx, valid = dsa_select(iq_b, iw_b, qpos, ik, topk)
        oa = sparse_attend(qa_b, rows_at(ka, idx), rows_at(va, idx), valid)
        ob = mla_attend(ql_b, qr_b, qpos, ckv, kr, w_uv)
        return jnp.concatenate([oa, ob], axis=-1)

    def to_blocks(a):
        return jnp.moveaxis(a.reshape((b, nb, QBLK) + a.shape[2:]), 1, 0)

    pos_blocks = jnp.arange(t, dtype=jnp.int32).reshape(nb, QBLK)
    out = lax.map(block, (to_blocks(qa), to_blocks(iq), to_blocks(iw),
                          to_blocks(q_lat), to_blocks(q_rope), pos_blocks))
    return jnp.moveaxis(out, 0, 1).reshape(b, t, MIX_W)


def sample_mix(qs, ks, l, cache_k_a, cache_v_a, cache_idx_k, cache_ckv, cache_krope, page_table, w_uv):
    qa, iq, iw, q_lat, q_rope = qs
    ka, va, ik, ckv, kr = ks
    b, t = qa.shape[:2]
    n_past = page_table.shape[1] * PAGE_SIZE
    qpos = n_past + jnp.arange(t, dtype=jnp.int32)
    topk = min(TOPK_MAX, (n_past + t) // 4)

    def past_rows(pool):
        g = pool[l, page_table]
        return g.reshape((b, n_past) + pool.shape[3:])

    ik_all = jnp.concatenate([past_rows(cache_idx_k), ik], axis=1)
    idx, valid = dsa_select(iq, iw, qpos, ik_all, topk)
    in_past = idx < n_past
    pidx = jnp.minimum(idx, n_past - 1)
    phys = rows_at(page_table, pidx // PAGE_SIZE)
    prow = pidx % PAGE_SIZE
    nidx = jnp.clip(idx - n_past, 0, t - 1)
    sel = in_past[..., None, None]
    k_sel = jnp.where(sel, cache_k_a[l, phys, prow], rows_at(ka, nidx))
    v_sel = jnp.where(sel, cache_v_a[l, phys, prow], rows_at(va, nidx))
    oa = sparse_attend(qa, k_sel, v_sel, valid)
    ckv_all = jnp.concatenate([past_rows(cache_ckv), ckv], axis=1)
    kr_all = jnp.concatenate([past_rows(cache_krope), kr], axis=1)
    ob = mla_attend(q_lat, q_rope, qpos, ckv_all, kr_all, w_uv)
    return jnp.concatenate([oa, ob], axis=-1)


def decoder_layer(x, c, pos, mix, conv_prev, w_ada, b_ada, g_attn, w_in, g_q, w_q_b, g_kv, w_uk,
                  w_o, g_ffn, w_up, conv_w, conv_b, w_down):
    mod = jnp.einsum('bd,de->be', jax.nn.silu(c), w_ada) + b_ada
    sh1, sc1, gt1, sh2, sc2, gt2 = [m[:, None, :] for m in jnp.split(mod, 6, axis=-1)]
    h = rms_norm(x, g_attn) * (1 + sc1) + sh1
    qs, ks = project_heads(h, pos, w_in, g_q, w_q_b, g_kv, w_uk)
    o = mix(qs, ks)
    x = x + gt1 * jnp.einsum('bte,ed->btd', o, w_o)
    h = rms_norm(x, g_ffn) * (1 + sc2) + sh2
    u = jnp.einsum('btd,df->btf', h, w_up)
    u_pad = jnp.concatenate([conv_prev.astype(u.dtype), u], axis=1)
    t = x.shape[1]
    v = conv_b
    for j in range(CONV_W):
        v = v + conv_w[j] * u_pad[:, j:j + t]
    a, g = jnp.split(v, 2, axis=-1)
    x = x + gt2 * jnp.einsum('btf,fd->btd', jax.nn.silu(a) * g, w_down)
    return x, ks, u_pad[:, -(CONV_W - 1):]


def setup_inputs(seed: int = 0) -> dict:
    key = jax.random.key(seed)
    keys = iter(jax.random.split(key, 32))

    def nrm(shape, scale):
        return jax.random.normal(next(keys), shape, jnp.float32) * scale

    def gain(shape):
        return 1.0 + nrm(shape, 0.02)

    n_pages = PAST_LEN // PAGE_SIZE
    n_used = DEC_BATCH * n_pages
    n_pool = n_used + n_used // 4
    f2 = 2 * D_FF
    page_table = jax.random.permutation(next(keys), n_pool)[:n_used].reshape(DEC_BATCH, n_pages).astype(jnp.int32)
    return {
        'x_prompt': nrm((BATCH, SEQ, D_MODEL), 1.0),
        'x_sample': nrm((DEC_BATCH, DEC_SEQ, D_MODEL), 1.0),
        'cache_k_a': nrm((DEPTH, n_pool, PAGE_SIZE, KV_A, DH_A), 1.0),
        'cache_v_a': nrm((DEPTH, n_pool, PAGE_SIZE, KV_A, DH_A), 1.0),
        'cache_idx_k': nrm((DEPTH, n_pool, PAGE_SIZE, D_IDX), 1.0),
        'cache_ckv': nrm((DEPTH, n_pool, PAGE_SIZE, KV_LORA), 1.0),
        'cache_krope': nrm((DEPTH, n_pool, PAGE_SIZE, DR_B), 1.0),
        'state_conv': nrm((DEPTH, DEC_BATCH, CONV_W - 1, f2), 1.0),
        'page_table': page_table,
        'c_prompt': nrm((BATCH, D_MODEL), 1.0),
        'c_sample': nrm((DEC_BATCH, D_MODEL), 1.0),
        'w_ada': nrm((DEPTH, D_MODEL, 6 * D_MODEL), D_MODEL ** -0.5),
        'b_ada': nrm((DEPTH, 6 * D_MODEL), 0.02),
        'g_attn': gain((DEPTH, D_MODEL)),
        'w_in': nrm((DEPTH, D_MODEL, IN_COLS), D_MODEL ** -0.5),
        'g_q': gain((DEPTH, Q_LORA)),
        'w_q_b': nrm((DEPTH, Q_LORA, H_B, DN_B + DR_B), Q_LORA ** -0.5),
        'g_kv': gain((DEPTH, KV_LORA)),
        'w_uk': nrm((DEPTH, KV_LORA, H_B, DN_B), KV_LORA ** -0.5),
        'w_uv': nrm((DEPTH, KV_LORA, H_B, DV_B), KV_LORA ** -0.5),
        'w_o': nrm((DEPTH, MIX_W, D_MODEL), MIX_W ** -0.5),
        'g_ffn': gain((DEPTH, D_MODEL)),
        'w_up': nrm((DEPTH, D_MODEL, f2), D_MODEL ** -0.5),
        'conv_w': nrm((DEPTH, CONV_W, f2), CONV_W ** -0.5),
        'conv_b': nrm((DEPTH, f2), 0.02),
        'w_down': nrm((DEPTH, D_FF, D_MODEL), D_FF ** -0.5),
        'g_final': gain((D_MODEL,)),
    }


def reference(x_prompt, x_sample, cache_k_a, cache_v_a, cache_idx_k, cache_ckv, cache_krope, state_conv,
              page_table, c_prompt, c_sample, w_ada, b_ada, g_attn, w_in, g_q, w_q_b, g_kv, w_uk, w_uv,
              w_o, g_ffn, w_up, conv_w, conv_b, w_down, g_final):
    pos_p = jnp.arange(x_prompt.shape[1], dtype=jnp.int32)
    n_past = page_table.shape[1] * PAGE_SIZE
    pos_s = n_past + jnp.arange(x_sample.shape[1], dtype=jnp.int32)
    xp, xs = x_prompt, x_sample
    new_p, new_s = [], []
    for l in range(DEPTH):
        lw = (w_ada[l], b_ada[l], g_attn[l], w_in[l], g_q[l], w_q_b[l], g_kv[l], w_uk[l],
              w_o[l], g_ffn[l], w_up[l], conv_w[l], conv_b[l], w_down[l])
        wuv = w_uv[l]
        conv0 = jnp.zeros((xp.shape[0], CONV_W - 1, 2 * D_FF), xp.dtype)

        def mix_p(qs, ks, wuv=wuv):
            return prompt_mix(qs, ks, wuv)

        def mix_s(qs, ks, l=l, wuv=wuv):
            return sample_mix(qs, ks, l, cache_k_a, cache_v_a, cache_idx_k, cache_ckv, cache_krope,
                              page_table, wuv)

        xp, kv_p, cs_p = decoder_layer(xp, c_prompt, pos_p, mix_p, conv0, *lw)
        xs, kv_s, cs_s = decoder_layer(xs, c_sample, pos_s, mix_s, state_conv[l], *lw)
        new_p.append(kv_p + (cs_p,))
        new_s.append(kv_s + (cs_s,))

    def stack(groups, i):
        return jnp.stack([g[i] for g in groups], axis=0)

    y_prompt = rms_norm(xp, g_final)
    y_sample = rms_norm(xs, g_final)
    return (y_prompt, y_sample,
            stack(new_p, 0), stack(new_p, 1), stack(new_p, 2), stack(new_p, 3), stack(new_p, 4), stack(new_p, 5),
            stack(new_s, 0), stack(new_s, 1), stack(new_s, 2), stack(new_s, 3), stack(new_s, 4), stack(new_s, 5))
```

```python
import functools

import jax
import jax.numpy as jnp
from jax import lax
from jax.experimental import pallas as pl
from jax.experimental.pallas import tpu as pltpu

F32 = jnp.float32
BF16 = jnp.bfloat16

H_A, KV_A, DH_A = 8, 2, 128
H_IDX, D_IDX = 16, 64
TOPK_MAX = 256
H_B, Q_LORA, KV_LORA, DN_B, DR_B, DV_B = 8, 512, 256, 128, 64, 128
CONV_W = 3
ROPE_THETA = 10000.0
EPS = 1e-6
PAGE_SIZE = 128

LANES = 128
NEG = -1e30
VMEM_LIMIT = 56 * 1024 * 1024

C_QA = 0
C_KA = C_QA + H_A * DH_A
C_VA = C_KA + KV_A * DH_A
C_IQ = C_VA + KV_A * DH_A
C_CQ = C_IQ + H_IDX * D_IDX
C_CKV = C_CQ + Q_LORA
C_IKKR = C_CKV + KV_LORA
C_IW = C_IKKR + LANES
C_END = C_IW + LANES

SEL_ITERS = 32


def _cparams(sem, vmem=VMEM_LIMIT):
    return pltpu.CompilerParams(dimension_semantics=sem, vmem_limit_bytes=vmem)


def _dot(a, b):
    return jnp.dot(a, b, preferred_element_type=F32)


def _dot_t(a, b):
    return lax.dot_general(a, b, (((1,), (1,)), ((), ())), preferred_element_type=F32)


def _rms(x, g):
    return x * lax.rsqrt(jnp.mean(x * x, axis=-1, keepdims=True) + EPS) * g


def _ada_kernel(c_ref, w_ref, b_ref, o_ref):
    c = c_ref[...]
    s = c * jax.nn.sigmoid(c)
    o_ref[...] = _dot(s.astype(BF16), w_ref[...].astype(BF16)) + b_ref[...]


def _ada(c_all, w_ada, b_ada):
    m, d = c_all.shape
    n = w_ada.shape[1]
    tn = 1024
    return pl.pallas_call(
        _ada_kernel,
        grid=(n // tn,),
        in_specs=[pl.BlockSpec((m, d), lambda j: (0, 0)),
                  pl.BlockSpec((d, tn), lambda j: (0, j)),
                  pl.BlockSpec((1, tn), lambda j: (0, j))],
        out_specs=pl.BlockSpec((m, tn), lambda j: (0, j)),
        out_shape=jax.ShapeDtypeStruct((m, n), F32),
        compiler_params=_cparams(("arbitrary",)),
        name="ada",
    )(c_all, w_ada, b_ada.reshape(1, n))


def _rope128(v, c, s):
    return v * c + pltpu.roll(v, 64, 1) * s


def _rope64(v, c, s):
    lane = lax.broadcasted_iota(jnp.int32, v.shape, 1)
    rot = jnp.where((lane & 63) < 32, pltpu.roll(v, 96, 1), pltpu.roll(v, 32, 1))
    return v * c + rot * s


def _proj_kernel(x_ref, sc_ref, sh_ref, g_ref, w_ref, c128_ref, s128_ref, c64_ref, s64_ref,
                 gq_ref, wqb_ref, wuk_ref, gkv_ref,
                 qa_ref, ka_ref, va_ref, iq_ref, ik_ref, iw_ref, ql_ref, qr_ref, ckv_ref, kr_ref):
    d = x_ref.shape[-1]
    x = x_ref[...].reshape(-1, d)
    sc = sc_ref[...].reshape(-1, d)
    sh = sh_ref[...].reshape(-1, d)
    h = _rms(x, g_ref[...]) * (1.0 + sc) + sh
    z = _dot(h.astype(BF16), w_ref[...])
    c128, s128 = c128_ref[...], s128_ref[...]
    c64, s64 = c64_ref[...], s64_ref[...]

    qdt = qa_ref.dtype
    qscale = DH_A ** -0.5
    for j in range(H_A):
        v = z[:, C_QA + j * LANES:C_QA + (j + 1) * LANES]
        qa_ref[:, j * LANES:(j + 1) * LANES] = (_rope128(v, c128, s128) * qscale).astype(qdt)
    for j in range(KV_A):
        v = z[:, C_KA + j * LANES:C_KA + (j + 1) * LANES]
        ka_ref[:, j * LANES:(j + 1) * LANES] = _rope128(v, c128, s128)
    va_ref[...] = z[:, C_VA:C_VA + KV_A * DH_A]
    iscale = D_IDX ** -0.5
    for j in range(H_IDX // 2):
        v = z[:, C_IQ + j * LANES:C_IQ + (j + 1) * LANES]
        r = _rope64(v, c64, s64) * iscale
        iq_ref[:, (2 * j) * LANES:(2 * j + 1) * LANES] = r.astype(qdt)
        iq_ref[:, (2 * j + 1) * LANES:(2 * j + 2) * LANES] = pltpu.roll(r, 64, 1).astype(qdt)
    r = _rope64(z[:, C_IKKR:C_IKKR + LANES], c64, s64)
    ik_ref[...] = r[:, :D_IDX]
    kr_ref[...] = pltpu.roll(r, 64, 1)[:, :DR_B]
    iw_ref[...] = z[:, C_IW:C_IW + H_IDX] * (H_IDX ** -0.5)
    ckv_ref[...] = _rms(z[:, C_CKV:C_CKV + KV_LORA], gkv_ref[...])
    cq = _rms(z[:, C_CQ:C_CQ + Q_LORA], gq_ref[...])
    qb = _dot(cq.astype(BF16), wqb_ref[...])
    bscale = (DN_B + DR_B) ** -0.5
    for hh in range(H_B):
        qn = qb[:, hh * DN_B:(hh + 1) * DN_B].astype(BF16)
        ql_ref[:, hh * KV_LORA:(hh + 1) * KV_LORA] = (_dot(qn, wuk_ref[hh]) * bscale).astype(qdt)
    for j in range(H_B // 2):
        off = H_B * DN_B + j * LANES
        r = _rope64(qb[:, off:off + LANES], c64, s64) * bscale
        qr_ref[:, (2 * j) * LANES:(2 * j + 1) * LANES] = r.astype(qdt)
        qr_ref[:, (2 * j + 1) * LANES:(2 * j + 2) * LANES] = pltpu.roll(r, 64, 1).astype(qdt)


def _proj(x, sc, sh, g, w_in2, tabs, gq, wqb2, wukt, gkv, *, tm, per_token_mod, qdt):
    d = x.shape[-1]
    if per_token_mod:
        n = x.shape[0]
        grid = (n // tm,)
        x_spec = pl.BlockSpec((tm, d), lambda i: (i, 0))
        m_spec = pl.BlockSpec((tm, d), lambda i: (i, 0))
        t_spec = pl.BlockSpec((tm, LANES), lambda i: (0, 0))
    else:
        b, t, _ = x.shape
        n = b * t
        tps = t // tm
        grid = (n // tm,)
        x_spec = pl.BlockSpec((1, tm, d), lambda i: (i // tps, i % tps, 0))
        m_spec = pl.BlockSpec((1, 1, d), lambda i: (i // tps, 0, 0))
        t_spec = pl.BlockSpec((tm, LANES), lambda i: (i % tps, 0))
    const = lambda *shape: pl.BlockSpec(shape, lambda i: (0,) * len(shape))
    row = lambda w: pl.BlockSpec((tm, w), lambda i: (i, 0))
    outs = [(H_A * DH_A, qdt), (KV_A * DH_A, F32), (KV_A * DH_A, F32), (H_IDX * LANES, qdt),
            (D_IDX, F32), (H_IDX, F32), (H_B * KV_LORA, qdt), (H_B * LANES, qdt),
            (KV_LORA, F32), (DR_B, F32)]
    return pl.pallas_call(
        _proj_kernel,
        grid=grid,
        in_specs=[x_spec, m_spec, m_spec, const(1, d), const(d, C_END),
                  t_spec, t_spec, t_spec, t_spec,
                  const(1, Q_LORA), const(Q_LORA, H_B * (DN_B + DR_B)),
                  const(H_B, DN_B, KV_LORA), const(1, KV_LORA)],
        out_specs=[row(w) for w, _ in outs],
        out_shape=[jax.ShapeDtypeStruct((n, w), dt) for w, dt in outs],
        compiler_params=_cparams(("arbitrary",)),
        name="proj",
    )(x, sc, sh, g, w_in2, *tabs, gq, wqb2, wukt, gkv)


def _select_bias(sc_ref, bias_ref, topk):
    rows, cols = sc_ref.shape
    x = sc_ref[...]
    valid = x > -jnp.inf
    nval = jnp.sum(jnp.where(valid, 1.0, 0.0), axis=-1, keepdims=True)
    kk = jnp.minimum(nval, float(topk))
    lo = jnp.min(jnp.where(valid, x, jnp.inf), axis=-1, keepdims=True)
    hi = jnp.max(x, axis=-1, keepdims=True)

    def vstep(_, c):
        lo, hi = c
        mid = 0.5 * (lo + hi)
        cnt = jnp.sum(jnp.where(sc_ref[...] >= mid, 1.0, 0.0), axis=-1, keepdims=True)
        ge = cnt >= kk
        return jnp.where(ge, mid, lo), jnp.where(ge, hi, mid)

    lo, hi = lax.fori_loop(0, SEL_ITERS, vstep, (lo, hi))
    vk = jnp.min(jnp.where(x >= lo, x, jnp.inf), axis=-1, keepdims=True)
    cgt = jnp.sum(jnp.where(x > vk, 1.0, 0.0), axis=-1, keepdims=True)
    need = kk - cgt
    idx = lax.broadcasted_iota(jnp.int32, (rows, cols), 1)
    jlo = jnp.full((rows, 1), -1, jnp.int32)
    jhi = jnp.full((rows, 1), cols - 1, jnp.int32)

    def istep(_, c):
        jlo, jhi = c
        jm = (jlo + jhi) >> 1
        xs = sc_ref[...]
        cnt = jnp.sum(jnp.where((xs == vk) & (idx <= jm), 1.0, 0.0), axis=-1, keepdims=True)
        ge = cnt >= need
        return jnp.where(ge, jlo, jm), jnp.where(ge, jm, jhi)

    n_isteps = max(1, (cols - 1).bit_length() + 1)
    jlo, jhi = lax.fori_loop(0, n_isteps, istep, (jlo, jhi))
    sel = (x > vk) | ((x == vk) & (idx <= jhi) & (need > 0.0))
    bias_ref[...] = jnp.where(sel, 0.0, NEG)


def _softmax_rows(s):
    m = jnp.max(s, axis=-1, keepdims=True)
    p = jnp.exp(s - m)
    return p, jnp.sum(p, axis=-1, keepdims=True)


def _pattn_kernel(qa_ref, iq_ref, iw_ref, ql_ref, qr_ref, ka_ref, va_ref, ik_ref, ckv_ref, kr_ref,
                  wuv_ref, o_ref, sc_scr, bias_scr, *, topk):
    tq = qa_ref.shape[1]
    s_len = ka_ref.shape[1]
    t0 = pl.program_id(1) * tq
    kpos = lax.broadcasted_iota(jnp.int32, (tq, s_len), 1)
    qpos = t0 + lax.broadcasted_iota(jnp.int32, (tq, s_len), 0)
    causal = kpos <= qpos

    ikb = ik_ref[0].astype(BF16)
    iw = iw_ref[0]
    score = jnp.zeros((tq, s_len), F32)
    for hh in range(H_IDX):
        qh = iq_ref[0, :, hh * LANES:hh * LANES + D_IDX]
        score = score + jnp.maximum(_dot_t(qh, ikb), 0.0) * iw[:, hh:hh + 1]
    sc_scr[...] = jnp.where(causal, score, -jnp.inf)
    _select_bias(sc_scr, bias_scr, topk)

    gsz = H_A // KV_A
    for g in range(KV_A):
        kg = ka_ref[0, :, g * DH_A:(g + 1) * DH_A].astype(BF16)
        vg = va_ref[0, :, g * DH_A:(g + 1) * DH_A].astype(BF16)
        for hl in range(gsz):
            hh = g * gsz + hl
            s = _dot_t(qa_ref[0, :, hh * DH_A:(hh + 1) * DH_A], kg) + bias_scr[...]
            p, l = _softmax_rows(s)
            o = _dot(p.astype(BF16), vg) / l
            o_ref[0, :, hh * DH_A:(hh + 1) * DH_A] = o.astype(o_ref.dtype)

    ckvb = ckv_ref[0].astype(BF16)
    krb = kr_ref[0].astype(BF16)
    cbias = jnp.where(causal, 0.0, NEG)
    base = H_A * DH_A
    for hh in range(H_B):
        s = (_dot_t(ql_ref[0, :, hh * KV_LORA:(hh + 1) * KV_LORA], ckvb)
             + _dot_t(qr_ref[0, :, hh * LANES:hh * LANES + DR_B], krb) + cbias)
        p, l = _softmax_rows(s)
        o_lat = _dot(p.astype(BF16), ckvb) / l
        o = _dot(o_lat.astype(BF16), wuv_ref[hh])
        o_ref[0, :, base + hh * DV_B:base + (hh + 1) * DV_B] = o.astype(o_ref.dtype)


def _pattn(qa, iq, iw, ql, qr, ka, va, ik, ckv, kr, wuvt, *, tq):
    b, s_len, _ = qa.shape
    topk = min(TOPK_MAX, s_len // 4)
    qspec = lambda w: pl.BlockSpec((1, tq, w), lambda bi, qi: (bi, qi, 0))
    kspec = lambda w: pl.BlockSpec((1, s_len, w), lambda bi, qi: (bi, 0, 0))
    mix = H_A * DH_A + H_B * DV_B
    return pl.pallas_call(
        functools.partial(_pattn_kernel, topk=topk),
        grid=(b, s_len // tq),
        in_specs=[qspec(H_A * DH_A), qspec(H_IDX * LANES), qspec(H_IDX), qspec(H_B * KV_LORA),
                  qspec(H_B * LANES), kspec(KV_A * DH_A), kspec(KV_A * DH_A), kspec(D_IDX),
                  kspec(KV_LORA), kspec(DR_B),
                  pl.BlockSpec((H_B, KV_LORA, DV_B), lambda bi, qi: (0, 0, 0))],
        out_specs=pl.BlockSpec((1, tq, mix), lambda bi, qi: (bi, qi, 0)),
        out_shape=jax.ShapeDtypeStruct((b, s_len, mix), BF16),
        scratch_shapes=[pltpu.VMEM((tq, s_len), F32), pltpu.VMEM((tq, s_len), F32)],
        compiler_params=_cparams(("arbitrary", "arbitrary")),
        name="pattn",
    )(qa, iq, iw, ql, qr, ka, va, ik, ckv, kr, wuvt)


def _page_copies(pt_ref, b, chunk, slot, pages_per_chunk, srcs, bufs, sems, layer):
    copies = []
    for p in range(pages_per_chunk):
        page = pt_ref[b, chunk * pages_per_chunk + p]
        for k, (src, buf) in enumerate(zip(srcs, bufs)):
            copies.append(pltpu.make_async_copy(
                src.at[layer, page], buf.at[slot, pl.ds(p * PAGE_SIZE, PAGE_SIZE)], sems.at[k, slot]))
    return copies


def _online_update(s, valid, v, m, l, acc):
    if valid is not None:
        s = jnp.where(valid, s, NEG)
    m_new = jnp.maximum(m, jnp.max(s, axis=-1, keepdims=True))
    p = jnp.exp(s - m_new)
    if valid is not None:
        p = jnp.where(valid, p, 0.0)
    alpha = jnp.exp(m - m_new)
    l = alpha * l + jnp.sum(p, axis=-1, keepdims=True)
    acc = alpha * acc + _dot(p.astype(BF16), v)
    return m_new, l, acc


def _stack_heads(ref, n_heads, stride, width):
    rows = [ref[:, hh * stride:hh * stride + width] for hh in range(n_heads)]
    return jnp.concatenate(rows, axis=0).astype(BF16)


def _pad_rows(ref, n):
    t, w = ref.shape
    return jnp.concatenate([ref[...], jnp.zeros((n - t, w), F32)], axis=0).astype(BF16)


def _smla_kernel(pt_ref, iq_ref, iw_ref, ql_ref, qr_ref, ikn_ref, ckvn_ref, krn_ref, wuv_ref,
                 cidx_hbm, cckv_hbm, ckr_hbm, sc_ref, ob_ref, ibuf, cbuf, rbuf, sems,
                 *, pages_per_chunk, n_chunks, layer):
    b = pl.program_id(0)
    t = iq_ref.shape[0]
    ck = pages_per_chunk * PAGE_SIZE
    n_past = n_chunks * ck
    srcs, bufs = (cidx_hbm, cckv_hbm, ckr_hbm), (ibuf, cbuf, rbuf)

    def start(chunk, slot):
        for c in _page_copies(pt_ref, b, chunk, slot, pages_per_chunk, srcs, bufs, sems, layer):
            c.start()

    def wait(chunk, slot):
        for c in _page_copies(pt_ref, b, chunk, slot, pages_per_chunk, srcs, bufs, sems, layer):
            c.wait()

    start(0, 0)
    iqs = _stack_heads(iq_ref, H_IDX, LANES, D_IDX)
    iw = iw_ref[...]
    qls = _stack_heads(ql_ref, H_B, KV_LORA, KV_LORA)
    qrs = _stack_heads(qr_ref, H_B, LANES, DR_B)
    rb = H_B * t

    def idx_scores(keys_bf):
        dd = jnp.maximum(_dot_t(iqs, keys_bf), 0.0)
        out = dd[0:t] * iw[:, 0:1]
        for hh in range(1, H_IDX):
            out = out + dd[hh * t:(hh + 1) * t] * iw[:, hh:hh + 1]
        return out

    def body(c, carry):
        m, l, acc = carry
        slot = c % 2

        @pl.when(c + 1 < n_chunks)
        def _():
            start(c + 1, 1 - slot)

        wait(c, slot)
        off = pl.multiple_of(c * ck, ck)
        sc_ref[0, :, pl.ds(off, ck)] = idx_scores(ibuf[slot].astype(BF16))
        ckvb = cbuf[slot].astype(BF16)
        s = _dot_t(qls, ckvb) + _dot_t(qrs, rbuf[slot].astype(BF16))
        return _online_update(s, None, ckvb, m, l, acc)

    init = (jnp.full((rb, 1), NEG, F32), jnp.zeros((rb, 1), F32), jnp.zeros((rb, KV_LORA), F32))
    m, l, acc = lax.fori_loop(0, n_chunks, body, init)

    pad = sc_ref.shape[2] - n_past
    kj = lax.broadcasted_iota(jnp.int32, (t, pad), 1)
    qi = lax.broadcasted_iota(jnp.int32, (t, pad), 0)
    sn = idx_scores(_pad_rows(ikn_ref, pad))
    sc_ref[0, :, pl.ds(n_past, pad)] = jnp.where(kj <= qi, sn, -jnp.inf)
    ckvn = _pad_rows(ckvn_ref, pad)
    s = _dot_t(qls, ckvn) + _dot_t(qrs, _pad_rows(krn_ref, pad))
    kj = lax.broadcasted_iota(jnp.int32, (rb, pad), 1)
    qi = lax.broadcasted_iota(jnp.int32, (rb, pad), 0) % t
    m, l, acc = _online_update(s, kj <= qi, ckvn, m, l, acc)
    o_lat = acc / l
    for hh in range(H_B):
        ob_ref[:, hh * DV_B:(hh + 1) * DV_B] = _dot(o_lat[hh * t:(hh + 1) * t].astype(BF16), wuv_ref[hh])


def _smla(page_table, iq, iw, ql, qr, ikn, ckvn, krn, wuvt, cache_idx_k, cache_ckv, cache_krope,
          *, t, layer, pages_per_chunk):
    nb, n_pages = page_table.shape
    n_chunks = n_pages // pages_per_chunk
    ck = pages_per_chunk * PAGE_SIZE
    n_past = n_pages * PAGE_SIZE
    tok = lambda w: pl.BlockSpec((t, w), lambda b, pt: (b, 0))
    any_spec = pl.BlockSpec(memory_space=pl.ANY)
    grid_spec = pltpu.PrefetchScalarGridSpec(
        num_scalar_prefetch=1,
        grid=(nb,),
        in_specs=[tok(H_IDX * LANES), tok(H_IDX), tok(H_B * KV_LORA), tok(H_B * LANES),
                  tok(D_IDX), tok(KV_LORA), tok(DR_B),
                  pl.BlockSpec((H_B, KV_LORA, DV_B), lambda b, pt: (0, 0, 0)),
                  any_spec, any_spec, any_spec],
        out_specs=[pl.BlockSpec((1, t, n_past + LANES), lambda b, pt: (b, 0, 0)),
                   pl.BlockSpec((t, H_B * DV_B), lambda b, pt: (b, 0))],
        scratch_shapes=[pltpu.VMEM((2, ck, D_IDX), F32), pltpu.VMEM((2, ck, KV_LORA), F32),
                        pltpu.VMEM((2, ck, DR_B), F32), pltpu.SemaphoreType.DMA((3, 2))],
    )
    return pl.pallas_call(
        functools.partial(_smla_kernel, pages_per_chunk=pages_per_chunk, n_chunks=n_chunks, layer=layer),
        grid_spec=grid_spec,
        out_shape=[jax.ShapeDtypeStruct((nb, t, n_past + LANES), F32),
                   jax.ShapeDtypeStruct((nb * t, H_B * DV_B), F32)],
        compiler_params=_cparams(("arbitrary",)),
        name="smla",
    )(page_table, iq, iw, ql, qr, ikn, ckvn, krn, wuvt, cache_idx_k, cache_ckv, cache_krope)


def _sdsa_kernel(pt_ref, sc_ref, qa_ref, kan_ref, van_ref, ck_hbm, cv_hbm, oa_ref,
                 kbuf, vbuf, sems, sc_scr, bias_scr, *, pages_per_chunk, n_chunks, layer, topk):
    b = pl.program_id(0)
    t = qa_ref.shape[0]
    ck = pages_per_chunk * PAGE_SIZE
    n_past = n_chunks * ck
    gsz = H_A // KV_A
    srcs, bufs = (ck_hbm, cv_hbm), (kbuf, vbuf)

    def start(chunk, slot):
        for c in _page_copies(pt_ref, b, chunk, slot, pages_per_chunk, srcs, bufs, sems, layer):
            c.start()

    def wait(chunk, slot):
        for c in _page_copies(pt_ref, b, chunk, slot, pages_per_chunk, srcs, bufs, sems, layer):
            c.wait()

    start(0, 0)
    sc_scr[...] = sc_ref[0]
    _select_bias(sc_scr, bias_scr, topk)
    qs = _stack_heads(qa_ref, H_A, DH_A, DH_A)
    rg = gsz * t

    def group_update(g, keys, vals, bias, carry):
        m, l, acc = carry
        s = _dot_t(qs[g * rg:(g + 1) * rg], keys)
        valid = jnp.concatenate([bias] * gsz, axis=0) > 0.5 * NEG
        return _online_update(s, valid, vals, m, l, acc)

    def body(c, carry):
        slot = c % 2

        @pl.when(c + 1 < n_chunks)
        def _():
            start(c + 1, 1 - slot)

        wait(c, slot)
        off = pl.multiple_of(c * ck, ck)
        bias = bias_scr[:, pl.ds(off, ck)]
        new = []
        for g in range(KV_A):
            keys = kbuf[slot, :, g, :].astype(BF16)
            vals = vbuf[slot, :, g, :].astype(BF16)
            new.append(group_update(g, keys, vals, bias, carry[g]))
        return tuple(new)

    init = tuple((jnp.full((rg, 1), NEG, F32), jnp.zeros((rg, 1), F32), jnp.zeros((rg, DH_A), F32))
                 for _ in range(KV_A))
    carry = lax.fori_loop(0, n_chunks, body, init)
    pad = sc_scr.shape[1] - n_past
    bias_n = bias_scr[:, pl.ds(n_past, pad)]
    for g in range(KV_A):
        keys = _pad_rows(kan_ref.at[:, g * DH_A:(g + 1) * DH_A], pad)
        vals = _pad_rows(van_ref.at[:, g * DH_A:(g + 1) * DH_A], pad)
        m, l, acc = group_update(g, keys, vals, bias_n, carry[g])
        o = acc / l
        for hl in range(gsz):
            hh = g * gsz + hl
            oa_ref[:, hh * DH_A:(hh + 1) * DH_A] = o[hl * t:(hl + 1) * t].astype(oa_ref.dtype)


def _sdsa(page_table, scores, qa, kan, van, cache_k_a, cache_v_a, *, t, layer, pages_per_chunk):
    nb, n_pages = page_table.shape
    n_chunks = n_pages // pages_per_chunk
    ck = pages_per_chunk * PAGE_SIZE
    n_past = n_pages * PAGE_SIZE
    topk = min(TOPK_MAX, (n_past + t) // 4)
    sp = scores.shape[2]
    tok = lambda w: pl.BlockSpec((t, w), lambda b, pt: (b, 0))
    any_spec = pl.BlockSpec(memory_space=pl.ANY)
    grid_spec = pltpu.PrefetchScalarGridSpec(
        num_scalar_prefetch=1,
        grid=(nb,),
        in_specs=[pl.BlockSpec((1, t, sp), lambda b, pt: (b, 0, 0)),
                  tok(H_A * DH_A), tok(KV_A * DH_A), tok(KV_A * DH_A), any_spec, any_spec],
        out_specs=tok(H_A * DH_A),
        scratch_shapes=[pltpu.VMEM((2, ck, KV_A, DH_A), F32), pltpu.VMEM((2, ck, KV_A, DH_A), F32),
                        pltpu.SemaphoreType.DMA((2, 2)),
                        pltpu.VMEM((t, sp), F32), pltpu.VMEM((t, sp), F32)],
    )
    return pl.pallas_call(
        functools.partial(_sdsa_kernel, pages_per_chunk=pages_per_chunk, n_chunks=n_chunks,
                          layer=layer, topk=topk),
        grid_spec=grid_spec,
        out_shape=jax.ShapeDtypeStruct((nb * t, H_A * DH_A), F32),
        compiler_params=_cparams(("arbitrary",)),
        name="sdsa",
    )(page_table, scores, qa, kan, van, cache_k_a, cache_v_a)


def _oproj_kernel(x_ref, o_ref, gt_ref, sc_ref, sh_ref, g_ref, w_ref, y_ref, h_ref):
    d = x_ref.shape[-1]
    x = x_ref[...].reshape(-1, d)
    o = o_ref[...].reshape(-1, o_ref.shape[-1]).astype(BF16)
    y = x + gt_ref[...].reshape(-1, d) * _dot(o, w_ref[...])
    y_ref[...] = y.reshape(y_ref.shape)
    h = _rms(y, g_ref[...]) * (1.0 + sc_ref[...].reshape(-1, d)) + sh_ref[...].reshape(-1, d)
    h_ref[...] = h.astype(h_ref.dtype).reshape(h_ref.shape)


def _tok_specs(x, tm, per_token_mod):
    d = x.shape[-1]
    if per_token_mod:
        spec = lambda w: pl.BlockSpec((tm, w), lambda i, *_: (i, 0))
        return x.shape[0] // tm, spec, spec(d), 1
    b, t, _ = x.shape
    tps = t // tm
    spec = lambda w: pl.BlockSpec((1, tm, w), lambda i, *_: (i // tps, i % tps, 0))
    return b * tps, spec, pl.BlockSpec((1, 1, d), lambda i, *_: (i // tps, 0, 0)), tps


def _oproj(x, o, gt, sc, sh, g, w_o, *, tm, per_token_mod):
    n_tiles, spec, m_spec, _ = _tok_specs(x, tm, per_token_mod)
    d = x.shape[-1]
    x_spec = spec(d)
    return pl.pallas_call(
        _oproj_kernel,
        grid=(n_tiles,),
        in_specs=[x_spec, spec(o.shape[-1]), m_spec, m_spec, m_spec,
                  pl.BlockSpec((1, d), lambda i: (0, 0)),
                  pl.BlockSpec(w_o.shape, lambda i: (0, 0))],
        out_specs=[x_spec, x_spec],
        out_shape=[jax.ShapeDtypeStruct(x.shape, F32), jax.ShapeDtypeStruct(x.shape, BF16)],
        compiler_params=_cparams(("arbitrary",)),
        name="oproj",
    )(x, o, gt, sc, sh, g, w_o)


def _ffn_kernel(*refs, per_token_mod, tps, t_dec, final_norm):
    if per_token_mod:
        (x_ref, h_ref, gt_ref, wa_ref, wg_ref, cw_ref, cb_ref, wd_ref, gf_ref,
         sa_ref, sg_ref, y_ref, conv_ref, acc_scr) = refs
    else:
        (x_ref, h_ref, gt_ref, wa_ref, wg_ref, cw_ref, cb_ref, wd_ref, gf_ref,
         y_ref, conv_ref, acc_scr, carry_scr) = refs
    d = x_ref.shape[-1]
    i, j = pl.program_id(0), pl.program_id(1)
    nj = pl.num_programs(1)
    tm = acc_scr.shape[0]

    @pl.when(j == 0)
    def _():
        acc_scr[...] = jnp.zeros_like(acc_scr)

    h = h_ref[...].reshape(-1, d)
    us = (_dot(h, wa_ref[...]), _dot(h, wg_ref[...]))
    cw = cw_ref[...]
    cb = cb_ref[...]
    vs = []
    if per_token_mod:
        nb = tm // t_dec
        tpos = lax.broadcasted_iota(jnp.int32, (nb, t_dec, us[0].shape[-1]), 1)
        for k, st_ref in enumerate((sa_ref, sg_ref)):
            u3 = us[k].reshape(nb, t_dec, -1)
            st = st_ref[...]
            um1 = pltpu.roll(jnp.where(tpos >= t_dec - 1, st, u3), 1, 1)
            um2 = pltpu.roll(jnp.where(tpos >= t_dec - 2, st, u3), 2, 1)
            v = cb[k] + cw[0, k] * um2 + cw[1, k] * um1 + cw[2, k] * u3
            vs.append(v.reshape(tm, -1))
            conv_ref[:, :, k, :] = u3[:, t_dec - (CONV_W - 1):, :]
    else:
        first = (i % tps) == 0
        rows = lax.broadcasted_iota(jnp.int32, us[0].shape, 0)
        for k in range(2):
            u = us[k]
            prev = jnp.where(first, 0.0, carry_scr[j, k])
            um1 = jnp.where(rows == 0, prev[1:2], pltpu.roll(u, 1, 0))
            um2 = jnp.where(rows == 0, prev[0:1], jnp.where(rows == 1, prev[1:2], pltpu.roll(u, 2, 0)))
            vs.append(cb[k] + cw[0, k] * um2 + cw[1, k] * um1 + cw[2, k] * u)
            tail = u[tm - (CONV_W - 1):, :]
            carry_scr[j, k] = tail
            conv_ref[0, :, k, :] = tail
    va, vg = vs
    act = (va * jax.nn.sigmoid(va)) * vg
    acc_scr[...] += _dot(act.astype(BF16), wd_ref[...])

    @pl.when(j == nj - 1)
    def _():
        x = x_ref[...].reshape(-1, d)
        x2 = x + gt_ref[...].reshape(-1, d) * acc_scr[...]
        if final_norm:
            x2 = _rms(x2, gf_ref[...])
        y_ref[...] = x2.reshape(y_ref.shape)


def _ffn(x, h, gt, w_up, conv_w, conv_b, w_down, g_final, state8, *, tm, tf, per_token_mod,
         t_dec, final_norm):
    n_tiles, spec, m_spec, tps = _tok_specs(x, tm, per_token_mod)
    d = x.shape[-1]
    x_spec = spec(d)
    f = w_down.shape[0]
    nj = f // tf
    const = lambda *shape: pl.BlockSpec(shape, lambda i, j: (0,) * len(shape))
    in_specs = [x_spec, x_spec, m_spec,
                pl.BlockSpec((d, tf), lambda i, j: (0, j)),
                pl.BlockSpec((d, tf), lambda i, j: (0, nj + j)),
                pl.BlockSpec((CONV_W, 2, tf), lambda i, j: (0, 0, j)),
                pl.BlockSpec((2, tf), lambda i, j: (0, j)),
                pl.BlockSpec((tf, d), lambda i, j: (j, 0)),
                const(1, d)]
    args = [x, h, gt, w_up, w_up, conv_w, conv_b, w_down, g_final]
    scratch = [pltpu.VMEM((tm, d), F32)]
    if per_token_mod:
        nb_blk = tm // t_dec
        n_entries = x.shape[0] // t_dec
        in_specs += [pl.BlockSpec((nb_blk, t_dec, tf), lambda i, j: (i, 0, j)),
                     pl.BlockSpec((nb_blk, t_dec, tf), lambda i, j: (i, 0, nj + j))]
        args += [state8, state8]
        conv_spec = pl.BlockSpec((nb_blk, CONV_W - 1, 2, tf), lambda i, j: (i, 0, 0, j))
    else:
        n_entries = n_tiles
        conv_spec = pl.BlockSpec((1, CONV_W - 1, 2, tf), lambda i, j: (i, 0, 0, j))
        scratch.append(pltpu.VMEM((nj, 2, CONV_W - 1, tf), F32))
    y, conv = pl.pallas_call(
        functools.partial(_ffn_kernel, per_token_mod=per_token_mod, tps=tps, t_dec=t_dec,
                          final_norm=final_norm),
        grid=(n_tiles, nj),
        in_specs=in_specs,
        out_specs=[x_spec, conv_spec],
        out_shape=[jax.ShapeDtypeStruct(x.shape, F32),
                   jax.ShapeDtypeStruct((n_entries, CONV_W - 1, 2, f), F32)],
        scratch_shapes=scratch,
        compiler_params=_cparams(("arbitrary", "arbitrary")),
        name="ffn",
    )(*args)
    return y, (conv if per_token_mod else conv[tps - 1::tps])


def _rope_tables(pos):
    tabs = []
    for half in (DH_A // 2, D_IDX // 2):
        inv = jnp.power(ROPE_THETA, -jnp.arange(half, dtype=F32) / half)
        ang = pos.astype(F32)[:, None] * inv[None, :]
        cos, sin = jnp.cos(ang), jnp.sin(ang)
        reps = LANES // (2 * half)
        tabs.append(jnp.tile(jnp.concatenate([cos, cos], axis=-1), (1, reps)))
        tabs.append(jnp.tile(jnp.concatenate([-sin, sin], axis=-1), (1, reps)))
    return tabs


def _relayout_w_in(w_in):
    sizes = (H_A * DH_A, KV_A * DH_A, KV_A * DH_A, H_IDX * D_IDX, D_IDX, H_IDX, Q_LORA, KV_LORA, DR_B)
    offs = [0]
    for s in sizes:
        offs.append(offs[-1] + s)
    qa, ka, va, iq, ik, iw, cq, ckv, kr = [w_in[:, offs[k]:offs[k + 1]] for k in range(len(sizes))]
    pad = jnp.zeros((w_in.shape[0], LANES - H_IDX), w_in.dtype)
    return jnp.concatenate([qa, ka, va, iq, cq, ckv, ik, kr, iw, pad], axis=1).astype(BF16)


def kernel(x_prompt, x_sample, cache_k_a, cache_v_a, cache_idx_k, cache_ckv, cache_krope, state_conv,
           page_table, c_prompt, c_sample, w_ada, b_ada, g_attn, w_in, g_q, w_q_b, g_kv, w_uk, w_uv,
           w_o, g_ffn, w_up, conv_w, conv_b, w_down, g_final):
    bp, seq, d = x_prompt.shape
    nb, t_dec, _ = x_sample.shape
    depth = w_ada.shape[0]
    f2 = w_up.shape[2]
    f = f2 // 2
    n_past = page_table.shape[1] * PAGE_SIZE
    n_s = nb * t_dec

    tabs_p = _rope_tables(jnp.arange(seq, dtype=jnp.int32))
    tm_s = min(256, n_s)
    pos_s = n_past + (jnp.arange(tm_s, dtype=jnp.int32) % t_dec)
    tabs_s = _rope_tables(pos_s)

    xp = x_prompt
    xs = x_sample.reshape(n_s, d)
    new_p, new_s = [], []
    c_rows = bp + nb
    c_pad = (-c_rows) % 8
    c_all = jnp.concatenate([c_prompt, c_sample, jnp.zeros((c_pad, d), F32)], axis=0)
    for l in range(depth):
        mod = _ada(c_all, w_ada[l], b_ada[l])
        mods = jnp.split(mod, 6, axis=-1)
        mp = [m[:bp].reshape(bp, 1, d) for m in mods]
        ms = [jnp.repeat(m[bp:bp + nb], t_dec, axis=0) for m in mods]
        w_in2 = _relayout_w_in(w_in[l])
        wqb = w_q_b[l]
        wqb2 = jnp.concatenate([wqb[:, :, :DN_B].reshape(Q_LORA, H_B * DN_B),
                                wqb[:, :, DN_B:].reshape(Q_LORA, H_B * DR_B)], axis=1).astype(BF16)
        wukt = jnp.transpose(w_uk[l], (1, 2, 0)).astype(BF16)
        wuvt = jnp.transpose(w_uv[l], (1, 0, 2)).astype(BF16)
        w_o_b = w_o[l].astype(BF16)
        w_up_b = w_up[l].astype(BF16)
        w_down_b = w_down[l].astype(BF16)
        cw = conv_w[l].reshape(CONV_W, 2, f)
        cb = conv_b[l].reshape(2, f)
        row = lambda v: v.reshape(1, -1)

        (qa, ka, va, iq, ik, iw, ql, qr, ckv, kr) = _proj(
            xp, mp[1], mp[0], row(g_attn[l]), w_in2, tabs_p, row(g_q[l]), wqb2, wukt, row(g_kv[l]),
            tm=256, per_token_mod=False, qdt=BF16)
        r3 = lambda a: a.reshape(bp, seq, a.shape[-1])
        o = _pattn(r3(qa), r3(iq), r3(iw), r3(ql), r3(qr), r3(ka), r3(va), r3(ik), r3(ckv), r3(kr),
                   wuvt, tq=128)
        last = l == depth - 1
        x1, h2 = _oproj(xp, o, mp[2], mp[4], mp[3], row(g_ffn[l]), w_o_b, tm=512, per_token_mod=False)
        xp, conv_p = _ffn(x1, h2, mp[5], w_up_b, cw, cb, w_down_b, row(g_final), None,
                          tm=512, tf=512, per_token_mod=False, t_dec=t_dec, final_norm=last)
        new_p.append((ka.reshape(bp, seq, KV_A, DH_A), va.reshape(bp, seq, KV_A, DH_A),
                      r3(ik), r3(ckv), r3(kr), conv_p.reshape(bp, CONV_W - 1, f2)))

        (qa, ka, va, iq, ik, iw, ql, qr, ckv, kr) = _proj(
            xs, ms[1], ms[0], row(g_attn[l]), w_in2, tabs_s, row(g_q[l]), wqb2, wukt, row(g_kv[l]),
            tm=tm_s, per_token_mod=True, qdt=F32)
        ppc = min(16, page_table.shape[1])
        scores, ob = _smla(page_table, iq, iw, ql, qr, ik, ckv, kr, wuvt,
                           cache_idx_k, cache_ckv, cache_krope, t=t_dec, layer=l, pages_per_chunk=ppc)
        oa = _sdsa(page_table, scores, qa, ka, va, cache_k_a, cache_v_a, t=t_dec, layer=l,
                   pages_per_chunk=ppc)
        o = jnp.concatenate([oa, ob], axis=-1)
        x1, h2 = _oproj(xs, o, ms[2], ms[4], ms[3], row(g_ffn[l]), w_o_b, tm=min(256, n_s),
                        per_token_mod=True)
        state8 = jnp.pad(state_conv[l], ((0, 0), (t_dec - (CONV_W - 1), 0), (0, 0)))
        xs, conv_s = _ffn(x1, h2, ms[5], w_up_b, cw, cb, w_down_b, row(g_final), state8,
                          tm=min(512, n_s), tf=512, per_token_mod=True, t_dec=t_dec, final_norm=last)
        r3s = lambda a: a.reshape(nb, t_dec, a.shape[-1])
        new_s.append((ka.reshape(nb, t_dec, KV_A, DH_A), va.reshape(nb, t_dec, KV_A, DH_A),
                      r3s(ik), r3s(ckv), r3s(kr), conv_s.reshape(nb, CONV_W - 1, f2)))

    def stack(groups, k):
        return jnp.stack([grp[k] for grp in groups], axis=0)

    return (xp, xs.reshape(nb, t_dec, d),
            stack(new_p, 0), stack(new_p, 1), stack(new_p, 2), stack(new_p, 3), stack(new_p, 4),
            stack(new_p, 5),
            stack(new_s, 0), stack(new_s, 1), stack(new_s, 2), stack(new_s, 3), stack(new_s, 4),
            stack(new_s, 5))
```

```python
import functools

import jax
import jax.numpy as jnp
from jax import lax
from jax.experimental import pallas as pl
from jax.experimental.pallas import tpu as pltpu

F32 = jnp.float32
BF16 = jnp.bfloat16

H_A, KV_A, DH_A = 8, 2, 128
H_IDX, D_IDX = 16, 64
TOPK_MAX = 256
H_B, Q_LORA, KV_LORA, DN_B, DR_B, DV_B = 8, 512, 256, 128, 64, 128
CONV_W = 3
ROPE_THETA = 10000.0
EPS = 1e-6
PAGE_SIZE = 128

LANES = 128
NEG = -1e30
VMEM_LIMIT = 56 * 1024 * 1024

C_QA = 0
C_KA = C_QA + H_A * DH_A
C_VA = C_KA + KV_A * DH_A
C_IQ = C_VA + KV_A * DH_A
C_CQ = C_IQ + H_IDX * D_IDX
C_CKV = C_CQ + Q_LORA
C_IKKR = C_CKV + KV_LORA
C_IW = C_IKKR + LANES
C_END = C_IW + LANES

FFN_SUB = 1
PAGES_PER_CHUNK = 32
SEL_ITERS = 26
SEL_EXTRA_STEPS = 4
SEL_EXTRA_ROUNDS = 48


def _cparams(sem, vmem=VMEM_LIMIT):
    return pltpu.CompilerParams(dimension_semantics=sem, vmem_limit_bytes=vmem)


def _dot(a, b):
    return jnp.dot(a, b, preferred_element_type=F32)


def _dot_t(a, b):
    return lax.dot_general(a, b, (((1,), (1,)), ((), ())), preferred_element_type=F32)


def _rms(x, g):
    return x * lax.rsqrt(jnp.mean(x * x, axis=-1, keepdims=True) + EPS) * g


def _ada_kernel(c_ref, w_ref, b_ref, o_ref):
    c = c_ref[...]
    s = c * jax.nn.sigmoid(c)
    o_ref[...] = _dot(s.astype(BF16), w_ref[...].astype(BF16)) + b_ref[...]


def _ada(c_all, w_ada, b_ada):
    m, d = c_all.shape
    n = w_ada.shape[1]
    tn = 1024
    return pl.pallas_call(
        _ada_kernel,
        grid=(n // tn,),
        in_specs=[pl.BlockSpec((m, d), lambda j: (0, 0)),
                  pl.BlockSpec((d, tn), lambda j: (0, j)),
                  pl.BlockSpec((1, tn), lambda j: (0, j))],
        out_specs=pl.BlockSpec((m, tn), lambda j: (0, j)),
        out_shape=jax.ShapeDtypeStruct((m, n), F32),
        compiler_params=_cparams(("arbitrary",)),
        name="ada",
    )(c_all, w_ada, b_ada.reshape(1, n))


def _rope128(v, c, s):
    return v * c + pltpu.roll(v, 64, 1) * s


def _rope64(v, c, s):
    lane = lax.broadcasted_iota(jnp.int32, v.shape, 1)
    rot = jnp.where((lane & 63) < 32, pltpu.roll(v, 96, 1), pltpu.roll(v, 32, 1))
    return v * c + rot * s


def _proj_kernel(x_ref, sc_ref, sh_ref, g_ref, w_ref, c128_ref, s128_ref, c64_ref, s64_ref,
                 gq_ref, wqb_ref, wuk_ref, gkv_ref,
                 qa_ref, ka_ref, va_ref, iq_ref, ik_ref, iw_ref, ql_ref, qr_ref, ckv_ref, kr_ref):
    d = x_ref.shape[-1]
    x = x_ref[...].reshape(-1, d)
    sc = sc_ref[...].reshape(-1, d)
    sh = sh_ref[...].reshape(-1, d)
    h = _rms(x, g_ref[...]) * (1.0 + sc) + sh
    z = _dot(h.astype(BF16), w_ref[...])
    c128, s128 = c128_ref[...], s128_ref[...]
    c64, s64 = c64_ref[...], s64_ref[...]

    qdt = qa_ref.dtype
    qscale = DH_A ** -0.5
    for j in range(H_A):
        v = z[:, C_QA + j * LANES:C_QA + (j + 1) * LANES]
        qa_ref[:, j * LANES:(j + 1) * LANES] = (_rope128(v, c128, s128) * qscale).astype(qdt)
    for j in range(KV_A):
        v = z[:, C_KA + j * LANES:C_KA + (j + 1) * LANES]
        ka_ref[:, j * LANES:(j + 1) * LANES] = _rope128(v, c128, s128)
    va_ref[...] = z[:, C_VA:C_VA + KV_A * DH_A]
    iscale = D_IDX ** -0.5
    for j in range(H_IDX // 2):
        v = z[:, C_IQ + j * LANES:C_IQ + (j + 1) * LANES]
        r = _rope64(v, c64, s64) * iscale
        iq_ref[:, (2 * j) * LANES:(2 * j + 1) * LANES] = r.astype(qdt)
        iq_ref[:, (2 * j + 1) * LANES:(2 * j + 2) * LANES] = pltpu.roll(r, 64, 1).astype(qdt)
    r = _rope64(z[:, C_IKKR:C_IKKR + LANES], c64, s64)
    ik_ref[...] = r[:, :D_IDX]
    kr_ref[...] = pltpu.roll(r, 64, 1)[:, :DR_B]
    iw_ref[...] = z[:, C_IW:C_IW + H_IDX] * (H_IDX ** -0.5)
    ckv_ref[...] = _rms(z[:, C_CKV:C_CKV + KV_LORA], gkv_ref[...])
    cq = _rms(z[:, C_CQ:C_CQ + Q_LORA], gq_ref[...])
    qb = _dot(cq.astype(BF16), wqb_ref[...])
    bscale = (DN_B + DR_B) ** -0.5
    for hh in range(H_B):
        qn = qb[:, hh * DN_B:(hh + 1) * DN_B].astype(BF16)
        ql_ref[:, hh * KV_LORA:(hh + 1) * KV_LORA] = (_dot(qn, wuk_ref[hh]) * bscale).astype(qdt)
    for j in range(H_B // 2):
        off = H_B * DN_B + j * LANES
        r = _rope64(qb[:, off:off + LANES], c64, s64) * bscale
        qr_ref[:, (2 * j) * LANES:(2 * j + 1) * LANES] = r.astype(qdt)
        qr_ref[:, (2 * j + 1) * LANES:(2 * j + 2) * LANES] = pltpu.roll(r, 64, 1).astype(qdt)


def _proj(x, sc, sh, g, w_in2, tabs, gq, wqb2, wukt, gkv, *, tm, per_token_mod, qdt):
    d = x.shape[-1]
    if per_token_mod:
        n = x.shape[0]
        grid = (n // tm,)
        x_spec = pl.BlockSpec((tm, d), lambda i: (i, 0))
        m_spec = pl.BlockSpec((tm, d), lambda i: (i, 0))
        t_spec = pl.BlockSpec((tm, LANES), lambda i: (0, 0))
    else:
        b, t, _ = x.shape
        n = b * t
        tps = t // tm
        grid = (n // tm,)
        x_spec = pl.BlockSpec((1, tm, d), lambda i: (i // tps, i % tps, 0))
        m_spec = pl.BlockSpec((1, 1, d), lambda i: (i // tps, 0, 0))
        t_spec = pl.BlockSpec((tm, LANES), lambda i: (i % tps, 0))
    const = lambda *shape: pl.BlockSpec(shape, lambda i: (0,) * len(shape))
    row = lambda w: pl.BlockSpec((tm, w), lambda i: (i, 0))
    outs = [(H_A * DH_A, qdt), (KV_A * DH_A, F32), (KV_A * DH_A, F32), (H_IDX * LANES, qdt),
            (D_IDX, F32), (H_IDX, F32), (H_B * KV_LORA, qdt), (H_B * LANES, qdt),
            (KV_LORA, F32), (DR_B, F32)]
    return pl.pallas_call(
        _proj_kernel,
        grid=grid,
        in_specs=[x_spec, m_spec, m_spec, const(1, d), const(d, C_END),
                  t_spec, t_spec, t_spec, t_spec,
                  const(1, Q_LORA), const(Q_LORA, H_B * (DN_B + DR_B)),
                  const(H_B, DN_B, KV_LORA), const(1, KV_LORA)],
        out_specs=[row(w) for w, _ in outs],
        out_shape=[jax.ShapeDtypeStruct((n, w), dt) for w, dt in outs],
        compiler_params=_cparams(("arbitrary",)),
        name="proj",
    )(x, sc, sh, g, w_in2, *tabs, gq, wqb2, wukt, gkv)


def _select_bias(sc_refs, bias_refs, topk, groups=1):
    probs = range(len(sc_refs))
    rows, cols = sc_refs[0].shape
    r = rows // groups

    def fold(v, op):
        if groups == 1:
            return v
        out = v[0:r]
        for g in range(1, groups):
            out = op(out, v[g * r:(g + 1) * r])
        return jnp.concatenate([out] * groups, axis=0)

    count = lambda m: fold(jnp.sum(jnp.where(m, 1.0, 0.0), axis=-1, keepdims=True), jnp.add)
    rmin = lambda v: fold(jnp.min(v, axis=-1, keepdims=True), jnp.minimum)
    rmax = lambda v: fold(jnp.max(v, axis=-1, keepdims=True), jnp.maximum)

    def anyrow(masks):
        flag = jnp.where(masks[0], 1.0, 0.0)
        for m in masks[1:]:
            flag = jnp.maximum(flag, jnp.where(m, 1.0, 0.0))
        return jnp.max(flag) > 0.5

    xs = [ref[...] for ref in sc_refs]
    valid = [x > -jnp.inf for x in xs]
    kk = [jnp.minimum(count(v), float(topk)) for v in valid]
    lo = tuple(rmin(jnp.where(v, x, jnp.inf)) for x, v in zip(xs, valid))
    hi = tuple(rmax(x) for x in xs)

    def vsteps(n, lo, hi):
        def step(_, c):
            lo, hi = c
            mid = [0.5 * (lo[p] + hi[p]) for p in probs]
            ge = [count(sc_refs[p][...] >= mid[p]) >= kk[p] for p in probs]
            return (tuple(jnp.where(ge[p], mid[p], lo[p]) for p in probs),
                    tuple(jnp.where(ge[p], hi[p], mid[p]) for p in probs))
        return lax.fori_loop(0, n, step, (lo, hi))

    def kth(lo):
        vk = tuple(rmin(jnp.where(sc_refs[p][...] >= lo[p], sc_refs[p][...], jnp.inf)) for p in probs)
        return vk, tuple(kk[p] - count(sc_refs[p][...] > vk[p]) for p in probs)

    lo, hi = vsteps(SEL_ITERS, lo, hi)
    vk, need = kth(lo)

    def more_cond(c):
        it, _, _, _, need = c
        return (it < SEL_EXTRA_ROUNDS) & anyrow([(need[p] <= 0.0) & (kk[p] > 0.0) for p in probs])

    def more_body(c):
        it, lo, hi, _, _ = c
        lo, hi = vsteps(SEL_EXTRA_STEPS, lo, hi)
        vk, need = kth(lo)
        return it + 1, lo, hi, vk, need

    _, lo, hi, vk, need = lax.while_loop(more_cond, more_body, (jnp.int32(0), lo, hi, vk, need))

    idx = lax.broadcasted_iota(jnp.int32, (rows, cols), 1)
    if groups > 1:
        idx = idx + (lax.broadcasted_iota(jnp.int32, (rows, cols), 0) // r) * cols
    last = groups * cols - 1

    def tie_break():
        def istep(_, c):
            jlo, jhi = c
            jm = [(jlo[p] + jhi[p]) >> 1 for p in probs]
            ge = [count((sc_refs[p][...] == vk[p]) & (idx <= jm[p])) >= need[p] for p in probs]
            return (tuple(jnp.where(ge[p], jlo[p], jm[p]) for p in probs),
                    tuple(jnp.where(ge[p], jm[p], jhi[p]) for p in probs))
        init = (tuple(jnp.full((rows, 1), -1, jnp.int32) for _ in probs),
                tuple(jnp.full((rows, 1), last, jnp.int32) for _ in probs))
        return lax.fori_loop(0, max(1, last.bit_length() + 1), istep, init)[1]

    jhi = lax.cond(anyrow([count(xs[p] == vk[p]) != need[p] for p in probs]), tie_break,
                   lambda: tuple(jnp.full((rows, 1), last, jnp.int32) for _ in probs))
    for p in probs:
        sel = (xs[p] > vk[p]) | ((xs[p] == vk[p]) & (idx <= jhi[p]) & (need[p] > 0.0))
        bias_refs[p][...] = jnp.where(sel, 0.0, NEG)


def _softmax_rows(s):
    m = jnp.max(s, axis=-1, keepdims=True)
    p = jnp.exp(s - m)
    return p, jnp.sum(p, axis=-1, keepdims=True)


def _pattn_kernel(qa_ref, iq_ref, iw_ref, ql_ref, qr_ref, ka_ref, va_ref, ik_ref, ckv_ref, kr_ref,
                  wuv_ref, o_ref, sc_scr, bias_scr, *, topk, n_var):
    tq = qa_ref.shape[1]
    s_len = ka_ref.shape[1]
    step = s_len // n_var
    variant = (pl.program_id(1) * tq) // step
    args = (qa_ref, iq_ref, iw_ref, ql_ref, qr_ref, ka_ref, va_ref, ik_ref, ckv_ref, kr_ref,
            wuv_ref, o_ref, sc_scr, bias_scr)
    for v in range(n_var):
        pl.when(variant == v)(functools.partial(_pattn_body, *args, topk=topk, nk=(v + 1) * step))


def _pattn_body(qa_ref, iq_ref, iw_ref, ql_ref, qr_ref, ka_ref, va_ref, ik_ref, ckv_ref, kr_ref,
                wuv_ref, o_ref, sc_scr, bias_scr, *, topk, nk):
    tq = qa_ref.shape[1]
    t0 = pl.program_id(1) * tq
    kpos = lax.broadcasted_iota(jnp.int32, (tq, nk), 1)
    qpos = t0 + lax.broadcasted_iota(jnp.int32, (tq, nk), 0)
    causal = kpos <= qpos
    sc_v = sc_scr.at[:, pl.ds(0, nk)]
    bias_v = bias_scr.at[:, pl.ds(0, nk)]

    ikb = ik_ref[0, :nk, :].astype(BF16)
    iw = iw_ref[0]
    score = jnp.zeros((tq, nk), F32)
    for hh in range(H_IDX):
        qh = iq_ref[0, :, hh * LANES:hh * LANES + D_IDX]
        score = score + jnp.maximum(_dot_t(qh, ikb), 0.0) * iw[:, hh:hh + 1]
    sc_v[...] = jnp.where(causal, score, -jnp.inf)
    _select_bias([sc_v], [bias_v], topk)

    gsz = H_A // KV_A
    for g in range(KV_A):
        kg = ka_ref[0, :nk, g * DH_A:(g + 1) * DH_A].astype(BF16)
        vg = va_ref[0, :nk, g * DH_A:(g + 1) * DH_A].astype(BF16)
        for hl in range(gsz):
            hh = g * gsz + hl
            s = _dot_t(qa_ref[0, :, hh * DH_A:(hh + 1) * DH_A], kg) + bias_v[...]
            p, l = _softmax_rows(s)
            o = _dot(p.astype(BF16), vg) / l
            o_ref[0, :, hh * DH_A:(hh + 1) * DH_A] = o.astype(o_ref.dtype)

    ckvb = ckv_ref[0, :nk, :].astype(BF16)
    krb = kr_ref[0, :nk, :].astype(BF16)
    cbias = jnp.where(causal, 0.0, NEG)
    base = H_A * DH_A
    for hh in range(H_B):
        s = (_dot_t(ql_ref[0, :, hh * KV_LORA:(hh + 1) * KV_LORA], ckvb)
             + _dot_t(qr_ref[0, :, hh * LANES:hh * LANES + DR_B], krb) + cbias)
        p, l = _softmax_rows(s)
        o_lat = _dot(p.astype(BF16), ckvb) / l
        o = _dot(o_lat.astype(BF16), wuv_ref[hh])
        o_ref[0, :, base + hh * DV_B:base + (hh + 1) * DV_B] = o.astype(o_ref.dtype)


def _pattn(qa, iq, iw, ql, qr, ka, va, ik, ckv, kr, wuvt, *, tq):
    b, s_len, _ = qa.shape
    topk = min(TOPK_MAX, s_len // 4)
    qspec = lambda w: pl.BlockSpec((1, tq, w), lambda bi, qi: (bi, qi, 0))
    kspec = lambda w: pl.BlockSpec((1, s_len, w), lambda bi, qi: (bi, 0, 0))
    mix = H_A * DH_A + H_B * DV_B
    n_var = max(1, min(4, s_len // tq))
    return pl.pallas_call(
        functools.partial(_pattn_kernel, topk=topk, n_var=n_var),
        grid=(b, s_len // tq),
        in_specs=[qspec(H_A * DH_A), qspec(H_IDX * LANES), qspec(H_IDX), qspec(H_B * KV_LORA),
                  qspec(H_B * LANES), kspec(KV_A * DH_A), kspec(KV_A * DH_A), kspec(D_IDX),
                  kspec(KV_LORA), kspec(DR_B),
                  pl.BlockSpec((H_B, KV_LORA, DV_B), lambda bi, qi: (0, 0, 0))],
        out_specs=pl.BlockSpec((1, tq, mix), lambda bi, qi: (bi, qi, 0)),
        out_shape=jax.ShapeDtypeStruct((b, s_len, mix), BF16),
        scratch_shapes=[pltpu.VMEM((tq, s_len), F32), pltpu.VMEM((tq, s_len), F32)],
        compiler_params=_cparams(("arbitrary", "arbitrary")),
        name="pattn",
    )(qa, iq, iw, ql, qr, ka, va, ik, ckv, kr, wuvt)


def _stream_chunks(pt_ref, srcs, bufs, lane_paged, sems, body_fn, init, *, layer, pages_per_chunk,
                   n_chunks):
    b, nb = pl.program_id(0), pl.num_programs(0)
    g0 = b * n_chunks

    def copies(bb, chunk, slot):
        out = []
        for p in range(pages_per_chunk):
            page = pt_ref[bb, chunk * pages_per_chunk + p]
            for k, (src, buf) in enumerate(zip(srcs, bufs)):
                r, c = src.shape[2], src.shape[3]
                dst = (buf.at[slot, :, pl.ds(p * c, c)] if lane_paged[k]
                       else buf.at[slot, pl.ds(p * r, r)])
                out.append(pltpu.make_async_copy(src.at[layer, page], dst, sems.at[k, slot]))
        return out

    def start(bb, chunk, slot):
        for cp in copies(bb, chunk, slot):
            cp.start()

    @pl.when(b == 0)
    def _():
        start(b, 0, 0)

    def body(c, carry):
        slot = (g0 + c) % 2

        @pl.when(c + 1 < n_chunks)
        def _():
            start(b, c + 1, 1 - slot)

        @pl.when((c + 1 == n_chunks) & (b + 1 < nb))
        def _():
            start(b + 1, 0, 1 - slot)

        for cp in copies(b, c, slot):
            cp.wait()
        return body_fn(c, slot, carry)

    return lax.fori_loop(0, n_chunks, body, init)


def _score_shape(nb, t, n_past, pages_per_chunk):
    ck = pages_per_chunk * PAGE_SIZE
    return (nb, (n_past // ck + 1) * t, ck)


def _online_update(s, valid, v, m, l, acc):
    if valid is not None:
        s = jnp.where(valid, s, NEG)
    m_new = jnp.maximum(m, jnp.max(s, axis=-1, keepdims=True))
    p = jnp.exp(s - m_new)
    if valid is not None:
        p = jnp.where(valid, p, 0.0)
    alpha = jnp.exp(m - m_new)
    l = alpha * l + jnp.sum(p, axis=-1, keepdims=True)
    acc = alpha * acc + _dot(p.astype(BF16), v)
    return m_new, l, acc


def _stack_heads(ref, n_heads, stride, width):
    rows = [ref[:, hh * stride:hh * stride + width] for hh in range(n_heads)]
    return jnp.concatenate(rows, axis=0).astype(BF16)


def _pad_rows(ref, n):
    t, w = ref.shape
    return jnp.concatenate([ref[...], jnp.zeros((n - t, w), F32)], axis=0).astype(BF16)


def _smla_kernel(pt_ref, iq_ref, iw_ref, ql_ref, qr_ref, ikn_ref, ckvn_ref, krn_ref, wuv_ref,
                 cidx_hbm, cckv_hbm, ckr_hbm, sc_ref, ob_ref, ibuf, cbuf, rbuf, sems,
                 *, pages_per_chunk, n_chunks, layer):
    t = iq_ref.shape[0]
    ck = pages_per_chunk * PAGE_SIZE
    n_past = n_chunks * ck
    iqs = _stack_heads(iq_ref, H_IDX, LANES, D_IDX)
    iw = iw_ref[...]
    qls = _stack_heads(ql_ref, H_B, KV_LORA, KV_LORA)
    qrs = _stack_heads(qr_ref, H_B, LANES, DR_B)
    rb = H_B * t

    def idx_scores(dots):
        dd = jnp.maximum(dots, 0.0)
        out = dd[0:t] * iw[:, 0:1]
        for hh in range(1, H_IDX):
            out = out + dd[hh * t:(hh + 1) * t] * iw[:, hh:hh + 1]
        return out

    def body(c, slot, carry):
        m, l, acc = carry
        sc_ref[0, pl.ds(pl.multiple_of(c * t, t), t), :] = idx_scores(_dot(iqs, ibuf[slot].astype(BF16)))
        ckvb = cbuf[slot].astype(BF16)
        s = _dot_t(qls, ckvb) + _dot(qrs, rbuf[slot].astype(BF16))
        return _online_update(s, None, ckvb, m, l, acc)

    init = (jnp.full((rb, 1), NEG, F32), jnp.zeros((rb, 1), F32), jnp.zeros((rb, KV_LORA), F32))
    m, l, acc = _stream_chunks(pt_ref, (cidx_hbm, cckv_hbm, ckr_hbm), (ibuf, cbuf, rbuf),
                               (True, False, True), sems, body, init, layer=layer,
                               pages_per_chunk=pages_per_chunk, n_chunks=n_chunks)

    pad = LANES
    kj = lax.broadcasted_iota(jnp.int32, (t, pad), 1)
    qi = lax.broadcasted_iota(jnp.int32, (t, pad), 0)
    sn = idx_scores(_dot_t(iqs, _pad_rows(ikn_ref, pad)))
    sn = jnp.where(kj <= qi, sn, -jnp.inf)
    sc_ref[0, n_chunks * t:(n_chunks + 1) * t, :] = jnp.concatenate(
        [sn, jnp.full((t, ck - pad), -jnp.inf, F32)], axis=1)
    ckvn = _pad_rows(ckvn_ref, pad)
    s = _dot_t(qls, ckvn) + _dot_t(qrs, _pad_rows(krn_ref, pad))
    kj = lax.broadcasted_iota(jnp.int32, (rb, pad), 1)
    qi = lax.broadcasted_iota(jnp.int32, (rb, pad), 0) % t
    m, l, acc = _online_update(s, kj <= qi, ckvn, m, l, acc)
    o_lat = acc / l
    for hh in range(H_B):
        ob_ref[:, hh * DV_B:(hh + 1) * DV_B] = _dot(o_lat[hh * t:(hh + 1) * t].astype(BF16), wuv_ref[hh])


def _smla(page_table, iq, iw, ql, qr, ikn, ckvn, krn, wuvt, cache_idx_k, cache_ckv, cache_krope,
          *, t, layer, pages_per_chunk):
    nb, n_pages = page_table.shape
    n_chunks = n_pages // pages_per_chunk
    ck = pages_per_chunk * PAGE_SIZE
    sc_shape = _score_shape(nb, t, n_pages * PAGE_SIZE, pages_per_chunk)
    tok = lambda w: pl.BlockSpec((t, w), lambda b, pt: (b, 0))
    any_spec = pl.BlockSpec(memory_space=pl.ANY)
    grid_spec = pltpu.PrefetchScalarGridSpec(
        num_scalar_prefetch=1,
        grid=(nb,),
        in_specs=[tok(H_IDX * LANES), tok(H_IDX), tok(H_B * KV_LORA), tok(H_B * LANES),
                  tok(D_IDX), tok(KV_LORA), tok(DR_B),
                  pl.BlockSpec((H_B, KV_LORA, DV_B), lambda b, pt: (0, 0, 0)),
                  any_spec, any_spec, any_spec],
        out_specs=[pl.BlockSpec((1,) + sc_shape[1:], lambda b, pt: (b, 0, 0)),
                   pl.BlockSpec((t, H_B * DV_B), lambda b, pt: (b, 0))],
        scratch_shapes=[pltpu.VMEM((2, D_IDX, ck), F32), pltpu.VMEM((2, ck, KV_LORA), F32),
                        pltpu.VMEM((2, DR_B, ck), F32), pltpu.SemaphoreType.DMA((3, 2))],
    )
    return pl.pallas_call(
        functools.partial(_smla_kernel, pages_per_chunk=pages_per_chunk, n_chunks=n_chunks, layer=layer),
        grid_spec=grid_spec,
        out_shape=[jax.ShapeDtypeStruct(sc_shape, F32),
                   jax.ShapeDtypeStruct((nb * t, H_B * DV_B), F32)],
        compiler_params=_cparams(("arbitrary",)),
        name="smla",
    )(page_table, iq, iw, ql, qr, ikn, ckvn, krn, wuvt, cache_idx_k, cache_ckv, cache_krope)


def _ssel_kernel(sc_ref, bias_ref, *, topk, groups):
    n = sc_ref.shape[0]
    _select_bias([sc_ref.at[e] for e in range(n)], [bias_ref.at[e] for e in range(n)], topk, groups)


def _ssel(scores, *, t, n_keys, per_step):
    nb, rows, ck = scores.shape
    spec = pl.BlockSpec((per_step, rows, ck), lambda i: (i, 0, 0))
    return pl.pallas_call(
        functools.partial(_ssel_kernel, topk=min(TOPK_MAX, n_keys // 4), groups=rows // t),
        grid=(nb // per_step,),
        in_specs=[spec],
        out_specs=spec,
        out_shape=jax.ShapeDtypeStruct(scores.shape, F32),
        compiler_params=_cparams(("arbitrary",)),
        name="ssel",
    )(scores)


def _sdsa_kernel(pt_ref, bias_ref, qa_ref, kan_ref, van_ref, ck_hbm, cv_hbm, oa_ref,
                 kbuf, vbuf, sems, *, pages_per_chunk, n_chunks, layer):
    t = qa_ref.shape[0]
    ck = pages_per_chunk * PAGE_SIZE
    gsz = H_A // KV_A
    bias_scr = bias_ref.at[0]
    qs = _stack_heads(qa_ref, H_A, DH_A, DH_A)
    rg = gsz * t

    def group_update(g, keys, vals, bias, carry):
        m, l, acc = carry
        s = _dot_t(qs[g * rg:(g + 1) * rg], keys) + jnp.concatenate([bias] * gsz, axis=0)
        return _online_update(s, None, vals, m, l, acc)

    def body(c, slot, carry):
        bias = bias_scr[pl.ds(pl.multiple_of(c * t, t), t), :]
        new = []
        for g in range(KV_A):
            keys = kbuf[slot, pl.ds(g, ck, stride=KV_A), :].astype(BF16)
            vals = vbuf[slot, pl.ds(g, ck, stride=KV_A), :].astype(BF16)
            new.append(group_update(g, keys, vals, bias, carry[g]))
        return tuple(new)

    init = tuple((jnp.full((rg, 1), NEG, F32), jnp.zeros((rg, 1), F32), jnp.zeros((rg, DH_A), F32))
                 for _ in range(KV_A))
    carry = _stream_chunks(pt_ref, (ck_hbm, cv_hbm), (kbuf, vbuf), (False, False), sems, body, init,
                           layer=layer, pages_per_chunk=pages_per_chunk, n_chunks=n_chunks)
    pad = LANES
    bias_n = bias_scr[n_chunks * t:(n_chunks + 1) * t, :pad]
    for g in range(KV_A):
        keys = _pad_rows(kan_ref.at[:, g * DH_A:(g + 1) * DH_A], pad)
        vals = _pad_rows(van_ref.at[:, g * DH_A:(g + 1) * DH_A], pad)
        m, l, acc = group_update(g, keys, vals, bias_n, carry[g])
        o = acc / l
        for hl in range(gsz):
            hh = g * gsz + hl
            oa_ref[:, hh * DH_A:(hh + 1) * DH_A] = o[hl * t:(hl + 1) * t].astype(oa_ref.dtype)


def _sdsa(page_table, bias, qa, kan, van, cache_k_a, cache_v_a, *, t, layer, pages_per_chunk):
    nb, n_pages = page_table.shape
    n_chunks = n_pages // pages_per_chunk
    ck = pages_per_chunk * PAGE_SIZE
    sc_blk = (1,) + bias.shape[1:]
    tok = lambda w: pl.BlockSpec((t, w), lambda b, pt: (b, 0))
    any_spec = pl.BlockSpec(memory_space=pl.ANY)
    grid_spec = pltpu.PrefetchScalarGridSpec(
        num_scalar_prefetch=1,
        grid=(nb,),
        in_specs=[pl.BlockSpec(sc_blk, lambda b, pt: (b, 0, 0)),
                  tok(H_A * DH_A), tok(KV_A * DH_A), tok(KV_A * DH_A), any_spec, any_spec],
        out_specs=tok(H_A * DH_A),
        scratch_shapes=[pltpu.VMEM((2, ck * KV_A, DH_A), F32), pltpu.VMEM((2, ck * KV_A, DH_A), F32),
                        pltpu.SemaphoreType.DMA((2, 2))],
    )
    return pl.pallas_call(
        functools.partial(_sdsa_kernel, pages_per_chunk=pages_per_chunk, n_chunks=n_chunks,
                          layer=layer),
        grid_spec=grid_spec,
        out_shape=jax.ShapeDtypeStruct((nb * t, H_A * DH_A), F32),
        compiler_params=_cparams(("arbitrary",)),
        name="sdsa",
    )(page_table, bias, qa, kan, van, cache_k_a, cache_v_a)


def _oproj_kernel(x_ref, o_ref, gt_ref, sc_ref, sh_ref, g_ref, w_ref, y_ref, h_ref):
    d = x_ref.shape[-1]
    x = x_ref[...].reshape(-1, d)
    o = o_ref[...].reshape(-1, o_ref.shape[-1]).astype(BF16)
    y = x + gt_ref[...].reshape(-1, d) * _dot(o, w_ref[...])
    y_ref[...] = y.reshape(y_ref.shape)
    h = _rms(y, g_ref[...]) * (1.0 + sc_ref[...].reshape(-1, d)) + sh_ref[...].reshape(-1, d)
    h_ref[...] = h.astype(h_ref.dtype).reshape(h_ref.shape)


def _tok_specs(x, tm, per_token_mod):
    d = x.shape[-1]
    if per_token_mod:
        spec = lambda w: pl.BlockSpec((tm, w), lambda i, *_: (i, 0))
        return x.shape[0] // tm, spec, spec(d), 1
    b, t, _ = x.shape
    tps = t // tm
    spec = lambda w: pl.BlockSpec((1, tm, w), lambda i, *_: (i // tps, i % tps, 0))
    return b * tps, spec, pl.BlockSpec((1, 1, d), lambda i, *_: (i // tps, 0, 0)), tps


def _oproj(x, o, gt, sc, sh, g, w_o, *, tm, per_token_mod):
    n_tiles, spec, m_spec, _ = _tok_specs(x, tm, per_token_mod)
    d = x.shape[-1]
    x_spec = spec(d)
    return pl.pallas_call(
        _oproj_kernel,
        grid=(n_tiles,),
        in_specs=[x_spec, spec(o.shape[-1]), m_spec, m_spec, m_spec,
                  pl.BlockSpec((1, d), lambda i: (0, 0)),
                  pl.BlockSpec(w_o.shape, lambda i: (0, 0))],
        out_specs=[x_spec, x_spec],
        out_shape=[jax.ShapeDtypeStruct(x.shape, F32), jax.ShapeDtypeStruct(x.shape, BF16)],
        compiler_params=_cparams(("arbitrary",)),
        name="oproj",
    )(x, o, gt, sc, sh, g, w_o)


def _ffn_kernel(*refs, per_token_mod, tps, t_dec, final_norm):
    if per_token_mod:
        (x_ref, h_ref, gt_ref, wa_ref, wg_ref, cw_ref, cb_ref, wd_ref, gf_ref,
         sa_ref, sg_ref, y_ref, conv_ref, acc_scr) = refs
    else:
        (x_ref, h_ref, gt_ref, wa_ref, wg_ref, cw_ref, cb_ref, wd_ref, gf_ref,
         y_ref, conv_ref, acc_scr, carry_scr) = refs
    d = x_ref.shape[-1]
    i, j = pl.program_id(0), pl.program_id(1)
    nj = pl.num_programs(1)
    tm = acc_scr.shape[0]

    @pl.when(j == 0)
    def _():
        acc_scr[...] = jnp.zeros_like(acc_scr)

    h = h_ref[...].reshape(-1, d)
    tf = wa_ref.shape[1]
    ts = tf // FFN_SUB
    delta = None
    for sb in range(FFN_SUB):
        cs = slice(sb * ts, (sb + 1) * ts)
        us = (_dot(h, wa_ref[:, cs]), _dot(h, wg_ref[:, cs]))
        cw = cw_ref[:, :, cs]
        cb = cb_ref[:, cs]
        vs = []
        if per_token_mod:
            nb = tm // t_dec
            tpos = lax.broadcasted_iota(jnp.int32, (nb, t_dec, ts), 1)
            for k, st_ref in enumerate((sa_ref, sg_ref)):
                u3 = us[k].reshape(nb, t_dec, ts)
                st = st_ref[:, :, cs]
                um1 = pltpu.roll(jnp.where(tpos >= t_dec - 1, st, u3), 1, 1)
                um2 = pltpu.roll(jnp.where(tpos >= t_dec - 2, st, u3), 2, 1)
                v = cb[k] + cw[0, k] * um2 + cw[1, k] * um1 + cw[2, k] * u3
                vs.append(v.reshape(tm, ts))
                conv_ref[:, :, k, cs] = u3[:, t_dec - (CONV_W - 1):, :]
        else:
            first = (i % tps) == 0
            rows = lax.broadcasted_iota(jnp.int32, (tm, ts), 0)
            for k in range(2):
                u = us[k]
                prev = jnp.where(first, 0.0, carry_scr[j, k, :, cs])
                um1 = jnp.where(rows == 0, prev[1:2], pltpu.roll(u, 1, 0))
                um2 = jnp.where(rows == 0, prev[0:1],
                                jnp.where(rows == 1, prev[1:2], pltpu.roll(u, 2, 0)))
                vs.append(cb[k] + cw[0, k] * um2 + cw[1, k] * um1 + cw[2, k] * u)
                tail = u[tm - (CONV_W - 1):, :]
                carry_scr[j, k, :, cs] = tail
                conv_ref[0, :, k, cs] = tail
        va, vg = vs
        act = (va * jax.nn.sigmoid(va)) * vg
        part = _dot(act.astype(BF16), wd_ref[cs, :])
        delta = part if delta is None else delta + part
    acc_scr[...] += delta

    @pl.when(j == nj - 1)
    def _():
        x = x_ref[...].reshape(-1, d)
        x2 = x + gt_ref[...].reshape(-1, d) * acc_scr[...]
        if final_norm:
            x2 = _rms(x2, gf_ref[...])
        y_ref[...] = x2.reshape(y_ref.shape)


def _ffn(x, h, gt, w_up, conv_w, conv_b, w_down, g_final, state8, *, tm, tf, per_token_mod,
         t_dec, final_norm):
    n_tiles, spec, m_spec, tps = _tok_specs(x, tm, per_token_mod)
    d = x.shape[-1]
    x_spec = spec(d)
    f = w_down.shape[0]
    nj = f // tf
    const = lambda *shape: pl.BlockSpec(shape, lambda i, j: (0,) * len(shape))
    in_specs = [x_spec, x_spec, m_spec,
                pl.BlockSpec((d, tf), lambda i, j: (0, j)),
                pl.BlockSpec((d, tf), lambda i, j: (0, nj + j)),
                pl.BlockSpec((CONV_W, 2, tf), lambda i, j: (0, 0, j)),
                pl.BlockSpec((2, tf), lambda i, j: (0, j)),
                pl.BlockSpec((tf, d), lambda i, j: (j, 0)),
                const(1, d)]
    args = [x, h, gt, w_up, w_up, conv_w, conv_b, w_down, g_final]
    scratch = [pltpu.VMEM((tm, d), F32)]
    if per_token_mod:
        nb_blk = tm // t_dec
        n_entries = x.shape[0] // t_dec
        in_specs += [pl.BlockSpec((nb_blk, t_dec, tf), lambda i, j: (i, 0, j)),
                     pl.BlockSpec((nb_blk, t_dec, tf), lambda i, j: (i, 0, nj + j))]
        args += [state8, state8]
        conv_spec = pl.BlockSpec((nb_blk, CONV_W - 1, 2, tf), lambda i, j: (i, 0, 0, j))
    else:
        n_entries = n_tiles
        conv_spec = pl.BlockSpec((1, CONV_W - 1, 2, tf), lambda i, j: (i, 0, 0, j))
        scratch.append(pltpu.VMEM((nj, 2, CONV_W - 1, tf), F32))
    y, conv = pl.pallas_call(
        functools.partial(_ffn_kernel, per_token_mod=per_token_mod, tps=tps, t_dec=t_dec,
                          final_norm=final_norm),
        grid=(n_tiles, nj),
        in_specs=in_specs,
        out_specs=[x_spec, conv_spec],
        out_shape=[jax.ShapeDtypeStruct(x.shape, F32),
                   jax.ShapeDtypeStruct((n_entries, CONV_W - 1, 2, f), F32)],
        scratch_shapes=scratch,
        compiler_params=_cparams(("arbitrary", "arbitrary")),
        name="ffn",
    )(*args)
    return y, (conv if per_token_mod else conv[tps - 1::tps])


def _rope_tables(pos):
    tabs = []
    for half in (DH_A // 2, D_IDX // 2):
        inv = jnp.power(ROPE_THETA, -jnp.arange(half, dtype=F32) / half)
        ang = pos.astype(F32)[:, None] * inv[None, :]
        cos, sin = jnp.cos(ang), jnp.sin(ang)
        reps = LANES // (2 * half)
        tabs.append(jnp.tile(jnp.concatenate([cos, cos], axis=-1), (1, reps)))
        tabs.append(jnp.tile(jnp.concatenate([-sin, sin], axis=-1), (1, reps)))
    return tabs


def _relayout_w_in(w_in):
    sizes = (H_A * DH_A, KV_A * DH_A, KV_A * DH_A, H_IDX * D_IDX, D_IDX, H_IDX, Q_LORA, KV_LORA, DR_B)
    offs = [0]
    for s in sizes:
        offs.append(offs[-1] + s)
    qa, ka, va, iq, ik, iw, cq, ckv, kr = [w_in[:, offs[k]:offs[k + 1]] for k in range(len(sizes))]
    pad = jnp.zeros((w_in.shape[0], LANES - H_IDX), w_in.dtype)
    return jnp.concatenate([qa, ka, va, iq, cq, ckv, ik, kr, iw, pad], axis=1).astype(BF16)


def kernel(x_prompt, x_sample, cache_k_a, cache_v_a, cache_idx_k, cache_ckv, cache_krope, state_conv,
           page_table, c_prompt, c_sample, w_ada, b_ada, g_attn, w_in, g_q, w_q_b, g_kv, w_uk, w_uv,
           w_o, g_ffn, w_up, conv_w, conv_b, w_down, g_final):
    bp, seq, d = x_prompt.shape
    nb, t_dec, _ = x_sample.shape
    depth = w_ada.shape[0]
    f2 = w_up.shape[2]
    f = f2 // 2
    n_past = page_table.shape[1] * PAGE_SIZE
    n_s = nb * t_dec

    tabs_p = _rope_tables(jnp.arange(seq, dtype=jnp.int32))
    tm_s = min(256, n_s)
    pos_s = n_past + (jnp.arange(tm_s, dtype=jnp.int32) % t_dec)
    tabs_s = _rope_tables(pos_s)

    cidx_t = jnp.swapaxes(cache_idx_k, 2, 3)
    ckr_t = jnp.swapaxes(cache_krope, 2, 3)
    n_pool = cache_k_a.shape[1]
    ck_rows = cache_k_a.reshape(depth, n_pool, PAGE_SIZE * KV_A, DH_A)
    cv_rows = cache_v_a.reshape(depth, n_pool, PAGE_SIZE * KV_A, DH_A)

    xp = x_prompt
    xs = x_sample.reshape(n_s, d)
    new_p, new_s = [], []
    c_rows = bp + nb
    c_pad = (-c_rows) % 8
    c_all = jnp.concatenate([c_prompt, c_sample, jnp.zeros((c_pad, d), F32)], axis=0)
    for l in range(depth):
        mod = _ada(c_all, w_ada[l], b_ada[l])
        mods = jnp.split(mod, 6, axis=-1)
        mp = [m[:bp].reshape(bp, 1, d) for m in mods]
        ms = [jnp.repeat(m[bp:bp + nb], t_dec, axis=0) for m in mods]
        w_in2 = _relayout_w_in(w_in[l])
        wqb = w_q_b[l]
        wqb2 = jnp.concatenate([wqb[:, :, :DN_B].reshape(Q_LORA, H_B * DN_B),
                                wqb[:, :, DN_B:].reshape(Q_LORA, H_B * DR_B)], axis=1).astype(BF16)
        wukt = jnp.transpose(w_uk[l], (1, 2, 0)).astype(BF16)
        wuvt = jnp.transpose(w_uv[l], (1, 0, 2)).astype(BF16)
        w_o_b = w_o[l].astype(BF16)
        w_up_b = w_up[l].astype(BF16)
        w_down_b = w_down[l].astype(BF16)
        cw = conv_w[l].reshape(CONV_W, 2, f)
        cb = conv_b[l].reshape(2, f)
        row = lambda v: v.reshape(1, -1)

        (qa, ka, va, iq, ik, iw, ql, qr, ckv, kr) = _proj(
            xp, mp[1], mp[0], row(g_attn[l]), w_in2, tabs_p, row(g_q[l]), wqb2, wukt, row(g_kv[l]),
            tm=256, per_token_mod=False, qdt=BF16)
        r3 = lambda a: a.reshape(bp, seq, a.shape[-1])
        o = _pattn(r3(qa), r3(iq), r3(iw), r3(ql), r3(qr), r3(ka), r3(va), r3(ik), r3(ckv), r3(kr),
                   wuvt, tq=128)
        last = l == depth - 1
        x1, h2 = _oproj(xp, o, mp[2], mp[4], mp[3], row(g_ffn[l]), w_o_b, tm=512, per_token_mod=False)
        xp, conv_p = _ffn(x1, h2, mp[5], w_up_b, cw, cb, w_down_b, row(g_final), None,
                          tm=512, tf=512, per_token_mod=False, t_dec=t_dec, final_norm=last)
        new_p.append((ka.reshape(bp, seq, KV_A, DH_A), va.reshape(bp, seq, KV_A, DH_A),
                      r3(ik), r3(ckv), r3(kr), conv_p.reshape(bp, CONV_W - 1, f2)))

        (qa, ka, va, iq, ik, iw, ql, qr, ckv, kr) = _proj(
            xs, ms[1], ms[0], row(g_attn[l]), w_in2, tabs_s, row(g_q[l]), wqb2, wukt, row(g_kv[l]),
            tm=tm_s, per_token_mod=True, qdt=F32)
        ppc = min(PAGES_PER_CHUNK, page_table.shape[1])
        scores, ob = _smla(page_table, iq, iw, ql, qr, ik, ckv, kr, wuvt,
                           cidx_t, cache_ckv, ckr_t, t=t_dec, layer=l, pages_per_chunk=ppc)
        bias = _ssel(scores, t=t_dec, n_keys=n_past + t_dec, per_step=4 if nb % 4 == 0 else 1)
        oa = _sdsa(page_table, bias, qa, ka, va, ck_rows, cv_rows, t=t_dec, layer=l,
                   pages_per_chunk=ppc)
        o = jnp.concatenate([oa, ob], axis=-1)
        x1, h2 = _oproj(xs, o, ms[2], ms[4], ms[3], row(g_ffn[l]), w_o_b, tm=min(256, n_s),
                        per_token_mod=True)
        state8 = jnp.pad(state_conv[l], ((0, 0), (t_dec - (CONV_W - 1), 0), (0, 0)))
        xs, conv_s = _ffn(x1, h2, ms[5], w_up_b, cw, cb, w_down_b, row(g_final), state8,
                          tm=min(512, n_s), tf=512, per_token_mod=True, t_dec=t_dec, final_norm=last)
        r3s = lambda a: a.reshape(nb, t_dec, a.shape[-1])
        new_s.append((ka.reshape(nb, t_dec, KV_A, DH_A), va.reshape(nb, t_dec, KV_A, DH_A),
                      r3s(ik), r3s(ckv), r3s(kr), conv_s.reshape(nb, CONV_W - 1, f2)))

    def stack(groups, k):
        return jnp.stack([grp[k] for grp in groups], axis=0)

    return (xp, xs.reshape(nb, t_dec, d),
            stack(new_p, 0), stack(new_p, 1), stack(new_p, 2), stack(new_p, 3), stack(new_p, 4),
            stack(new_p, 5),
            stack(new_s, 0), stack(new_s, 1), stack(new_s, 2), stack(new_s, 3), stack(new_s, 4),
            stack(new_s, 5))
```

```python
import functools

import jax
import jax.numpy as jnp
from jax import lax
from jax.experimental import pallas as pl
from jax.experimental.pallas import tpu as pltpu

F32 = jnp.float32
BF16 = jnp.bfloat16

H_A, KV_A, DH_A = 8, 2, 128
H_IDX, D_IDX = 16, 64
TOPK_MAX = 256
H_B, Q_LORA, KV_LORA, DN_B, DR_B, DV_B = 8, 512, 256, 128, 64, 128
CONV_W = 3
ROPE_THETA = 10000.0
EPS = 1e-6
PAGE_SIZE = 128

LANES = 128
NEG = -1e30
VMEM_LIMIT = 56 * 1024 * 1024

C_QA = 0
C_KA = C_QA + H_A * DH_A
C_VA = C_KA + KV_A * DH_A
C_IQ = C_VA + KV_A * DH_A
C_CQ = C_IQ + H_IDX * D_IDX
C_CKV = C_CQ + Q_LORA
C_IKKR = C_CKV + KV_LORA
C_IW = C_IKKR + LANES
C_END = C_IW + LANES

FFN_SUB = 1
PATTN_STACK = 4
SAMPLE_ENTRIES = 2
PAGES_PER_CHUNK = 32
SEL_ITERS = 24
SEL_EXTRA_STEPS = 4
SEL_EXTRA_ROUNDS = 48


def _cparams(sem, vmem=VMEM_LIMIT):
    return pltpu.CompilerParams(dimension_semantics=sem, vmem_limit_bytes=vmem)


def _dot(a, b):
    return jnp.dot(a, b, preferred_element_type=F32)


def _dot_t(a, b):
    return lax.dot_general(a, b, (((1,), (1,)), ((), ())), preferred_element_type=F32)


def _rms(x, g):
    return x * lax.rsqrt(jnp.mean(x * x, axis=-1, keepdims=True) + EPS) * g


def _ada_kernel(c_ref, w_ref, b_ref, o_ref):
    c = c_ref[...]
    s = c * jax.nn.sigmoid(c)
    o_ref[...] = _dot(s.astype(BF16), w_ref[...].astype(BF16)) + b_ref[...]


def _ada(c_all, w_ada, b_ada):
    m, d = c_all.shape
    n = w_ada.shape[1]
    tn = 1024
    return pl.pallas_call(
        _ada_kernel,
        grid=(n // tn,),
        in_specs=[pl.BlockSpec((m, d), lambda j: (0, 0)),
                  pl.BlockSpec((d, tn), lambda j: (0, j)),
                  pl.BlockSpec((1, tn), lambda j: (0, j))],
        out_specs=pl.BlockSpec((m, tn), lambda j: (0, j)),
        out_shape=jax.ShapeDtypeStruct((m, n), F32),
        compiler_params=_cparams(("arbitrary",)),
        name="ada",
    )(c_all, w_ada, b_ada.reshape(1, n))


def _rope128(v, c, s):
    return v * c + pltpu.roll(v, 64, 1) * s


def _rope64(v, c, s):
    lane = lax.broadcasted_iota(jnp.int32, v.shape, 1)
    rot = jnp.where((lane & 63) < 32, pltpu.roll(v, 96, 1), pltpu.roll(v, 32, 1))
    return v * c + rot * s


def _proj_kernel(x_ref, sc_ref, sh_ref, g_ref, w_ref, c128_ref, s128_ref, c64_ref, s64_ref,
                 gq_ref, wqb_ref, wuk_ref, gkv_ref,
                 qa_ref, ka_ref, va_ref, iq_ref, ik_ref, iw_ref, ql_ref, qr_ref, ckv_ref, kr_ref):
    d = x_ref.shape[-1]
    x = x_ref[...].reshape(-1, d)
    sc = sc_ref[...].reshape(-1, d)
    sh = sh_ref[...].reshape(-1, d)
    h = _rms(x, g_ref[...]) * (1.0 + sc) + sh
    z = _dot(h.astype(BF16), w_ref[...])
    c128, s128 = c128_ref[...], s128_ref[...]
    c64, s64 = c64_ref[...], s64_ref[...]

    qdt = qa_ref.dtype
    qscale = DH_A ** -0.5
    for j in range(H_A):
        v = z[:, C_QA + j * LANES:C_QA + (j + 1) * LANES]
        qa_ref[:, j * LANES:(j + 1) * LANES] = (_rope128(v, c128, s128) * qscale).astype(qdt)
    for j in range(KV_A):
        v = z[:, C_KA + j * LANES:C_KA + (j + 1) * LANES]
        ka_ref[:, j * LANES:(j + 1) * LANES] = _rope128(v, c128, s128)
    va_ref[...] = z[:, C_VA:C_VA + KV_A * DH_A]
    iscale = D_IDX ** -0.5
    for j in range(H_IDX // 2):
        v = z[:, C_IQ + j * LANES:C_IQ + (j + 1) * LANES]
        r = _rope64(v, c64, s64) * iscale
        iq_ref[:, (2 * j) * LANES:(2 * j + 1) * LANES] = r.astype(qdt)
        iq_ref[:, (2 * j + 1) * LANES:(2 * j + 2) * LANES] = pltpu.roll(r, 64, 1).astype(qdt)
    r = _rope64(z[:, C_IKKR:C_IKKR + LANES], c64, s64)
    ik_ref[...] = r[:, :D_IDX]
    kr_ref[...] = pltpu.roll(r, 64, 1)[:, :DR_B]
    iw_ref[...] = z[:, C_IW:C_IW + H_IDX] * (H_IDX ** -0.5)
    ckv_ref[...] = _rms(z[:, C_CKV:C_CKV + KV_LORA], gkv_ref[...])
    cq = _rms(z[:, C_CQ:C_CQ + Q_LORA], gq_ref[...])
    qb = _dot(cq.astype(BF16), wqb_ref[...])
    bscale = (DN_B + DR_B) ** -0.5
    for hh in range(H_B):
        qn = qb[:, hh * DN_B:(hh + 1) * DN_B].astype(BF16)
        ql_ref[:, hh * KV_LORA:(hh + 1) * KV_LORA] = (_dot(qn, wuk_ref[hh]) * bscale).astype(qdt)
    for j in range(H_B // 2):
        off = H_B * DN_B + j * LANES
        r = _rope64(qb[:, off:off + LANES], c64, s64) * bscale
        qr_ref[:, (2 * j) * LANES:(2 * j + 1) * LANES] = r.astype(qdt)
        qr_ref[:, (2 * j + 1) * LANES:(2 * j + 2) * LANES] = pltpu.roll(r, 64, 1).astype(qdt)


def _proj(x, sc, sh, g, w_in2, tabs, gq, wqb2, wukt, gkv, *, tm, per_token_mod, qdt):
    d = x.shape[-1]
    if per_token_mod:
        n = x.shape[0]
        grid = (n // tm,)
        x_spec = pl.BlockSpec((tm, d), lambda i: (i, 0))
        m_spec = pl.BlockSpec((tm, d), lambda i: (i, 0))
        t_spec = pl.BlockSpec((tm, LANES), lambda i: (0, 0))
    else:
        b, t, _ = x.shape
        n = b * t
        tps = t // tm
        grid = (n // tm,)
        x_spec = pl.BlockSpec((1, tm, d), lambda i: (i // tps, i % tps, 0))
        m_spec = pl.BlockSpec((1, 1, d), lambda i: (i // tps, 0, 0))
        t_spec = pl.BlockSpec((tm, LANES), lambda i: (i % tps, 0))
    const = lambda *shape: pl.BlockSpec(shape, lambda i: (0,) * len(shape))
    row = lambda w: pl.BlockSpec((tm, w), lambda i: (i, 0))
    outs = [(H_A * DH_A, qdt), (KV_A * DH_A, F32), (KV_A * DH_A, F32), (H_IDX * LANES, qdt),
            (D_IDX, F32), (H_IDX, F32), (H_B * KV_LORA, qdt), (H_B * LANES, qdt),
            (KV_LORA, F32), (DR_B, F32)]
    return pl.pallas_call(
        _proj_kernel,
        grid=grid,
        in_specs=[x_spec, m_spec, m_spec, const(1, d), const(d, C_END),
                  t_spec, t_spec, t_spec, t_spec,
                  const(1, Q_LORA), const(Q_LORA, H_B * (DN_B + DR_B)),
                  const(H_B, DN_B, KV_LORA), const(1, KV_LORA)],
        out_specs=[row(w) for w, _ in outs],
        out_shape=[jax.ShapeDtypeStruct((n, w), dt) for w, dt in outs],
        compiler_params=_cparams(("arbitrary",)),
        name="proj",
    )(x, sc, sh, g, w_in2, *tabs, gq, wqb2, wukt, gkv)


def _select_bias(sc_refs, bias_refs, topk, groups=1):
    probs = range(len(sc_refs))
    rows, cols = sc_refs[0].shape
    r = rows // groups

    def fold(v, op):
        if groups == 1:
            return v
        out = v[0:r]
        for g in range(1, groups):
            out = op(out, v[g * r:(g + 1) * r])
        return jnp.concatenate([out] * groups, axis=0)

    count = lambda m: fold(jnp.sum(jnp.where(m, 1.0, 0.0), axis=-1, keepdims=True), jnp.add)
    rmin = lambda v: fold(jnp.min(v, axis=-1, keepdims=True), jnp.minimum)
    rmax = lambda v: fold(jnp.max(v, axis=-1, keepdims=True), jnp.maximum)

    def anyrow(masks):
        flag = jnp.where(masks[0], 1.0, 0.0)
        for m in masks[1:]:
            flag = jnp.maximum(flag, jnp.where(m, 1.0, 0.0))
        return jnp.max(flag) > 0.5

    xs = [ref[...] for ref in sc_refs]
    valid = [x > -jnp.inf for x in xs]
    kk = [jnp.minimum(count(v), float(topk)) for v in valid]
    lo = tuple(rmin(jnp.where(v, x, jnp.inf)) for x, v in zip(xs, valid))
    hi = tuple(rmax(x) for x in xs)

    def vsteps(n, lo, hi):
        def step(_, c):
            lo, hi = c
            mid = [0.5 * (lo[p] + hi[p]) for p in probs]
            ge = [count(sc_refs[p][...] >= mid[p]) >= kk[p] for p in probs]
            return (tuple(jnp.where(ge[p], mid[p], lo[p]) for p in probs),
                    tuple(jnp.where(ge[p], hi[p], mid[p]) for p in probs))
        return lax.fori_loop(0, n, step, (lo, hi))

    def kth(lo):
        vk = tuple(rmin(jnp.where(sc_refs[p][...] >= lo[p], sc_refs[p][...], jnp.inf)) for p in probs)
        return vk, tuple(kk[p] - count(sc_refs[p][...] > vk[p]) for p in probs)

    lo, hi = vsteps(SEL_ITERS, lo, hi)
    vk, need = kth(lo)

    def more_cond(c):
        it, _, _, _, need = c
        return (it < SEL_EXTRA_ROUNDS) & anyrow([(need[p] <= 0.0) & (kk[p] > 0.0) for p in probs])

    def more_body(c):
        it, lo, hi, _, _ = c
        lo, hi = vsteps(SEL_EXTRA_STEPS, lo, hi)
        vk, need = kth(lo)
        return it + 1, lo, hi, vk, need

    _, lo, hi, vk, need = lax.while_loop(more_cond, more_body, (jnp.int32(0), lo, hi, vk, need))

    idx = lax.broadcasted_iota(jnp.int32, (rows, cols), 1)
    if groups > 1:
        idx = idx + (lax.broadcasted_iota(jnp.int32, (rows, cols), 0) // r) * cols
    last = groups * cols - 1

    def tie_break():
        def istep(_, c):
            jlo, jhi = c
            jm = [(jlo[p] + jhi[p]) >> 1 for p in probs]
            ge = [count((sc_refs[p][...] == vk[p]) & (idx <= jm[p])) >= need[p] for p in probs]
            return (tuple(jnp.where(ge[p], jlo[p], jm[p]) for p in probs),
                    tuple(jnp.where(ge[p], jm[p], jhi[p]) for p in probs))
        init = (tuple(jnp.full((rows, 1), -1, jnp.int32) for _ in probs),
                tuple(jnp.full((rows, 1), last, jnp.int32) for _ in probs))
        return lax.fori_loop(0, max(1, last.bit_length() + 1), istep, init)[1]

    jhi = lax.cond(anyrow([count(xs[p] == vk[p]) != need[p] for p in probs]), tie_break,
                   lambda: tuple(jnp.full((rows, 1), last, jnp.int32) for _ in probs))
    for p in probs:
        sel = (xs[p] > vk[p]) | ((xs[p] == vk[p]) & (idx <= jhi[p]) & (need[p] > 0.0))
        bias_refs[p][...] = jnp.where(sel, 0.0, NEG)


def _pattn_kernel(qa_ref, iq_ref, iw_ref, ql_ref, qr_ref, ka_ref, va_ref, ik_ref, ckv_ref, kr_ref,
                  wuv_ref, o_ref, sc_scr, bias_scr, *, topk, q_off):
    tq = qa_ref.shape[1]
    nk = ka_ref.shape[1]
    t0 = (q_off + pl.program_id(1)) * tq
    kpos = lax.broadcasted_iota(jnp.int32, (tq, nk), 1)
    qpos = t0 + lax.broadcasted_iota(jnp.int32, (tq, nk), 0)
    causal = kpos <= qpos
    sc_v, bias_v = sc_scr, bias_scr

    ikb = ik_ref[0].astype(BF16)
    iw = iw_ref[0]
    score = jnp.zeros((tq, nk), F32)
    for hh in range(H_IDX):
        qh = iq_ref[0, :, hh * LANES:hh * LANES + D_IDX]
        score = score + jnp.maximum(_dot_t(qh, ikb), 0.0) * iw[:, hh:hh + 1]
    sc_v[...] = jnp.where(causal, score, -jnp.inf)
    _select_bias([sc_v], [bias_v], topk)

    def attend(q_rows, n_stack, logits_fn, bias, values):
        s = logits_fn(q_rows).reshape(n_stack, tq, nk) + bias[None]
        m = jnp.max(s, axis=-1, keepdims=True)
        p = jnp.exp(s - m)
        l = jnp.sum(p, axis=-1, keepdims=True)
        o = _dot(p.reshape(n_stack * tq, nk).astype(BF16), values)
        return o / l.reshape(n_stack * tq, 1)

    gsz = H_A // KV_A
    bias_a = bias_v[...]
    for g in range(KV_A):
        kg = ka_ref[0, :, g * DH_A:(g + 1) * DH_A].astype(BF16)
        vg = va_ref[0, :, g * DH_A:(g + 1) * DH_A].astype(BF16)
        heads = range(g * gsz, (g + 1) * gsz)
        qg = jnp.concatenate([qa_ref[0, :, hh * DH_A:(hh + 1) * DH_A] for hh in heads], axis=0)
        o = attend(qg, gsz, lambda q: _dot_t(q, kg), bias_a, vg)
        for j, hh in enumerate(heads):
            o_ref[0, :, hh * DH_A:(hh + 1) * DH_A] = o[j * tq:(j + 1) * tq].astype(o_ref.dtype)

    ckvb = ckv_ref[0].astype(BF16)
    krb = kr_ref[0].astype(BF16)
    cbias = jnp.where(causal, 0.0, NEG)
    base = H_A * DH_A
    for h0 in range(0, H_B, PATTN_STACK):
        heads = range(h0, h0 + PATTN_STACK)
        qlat = jnp.concatenate([ql_ref[0, :, hh * KV_LORA:(hh + 1) * KV_LORA] for hh in heads], axis=0)
        qrope = jnp.concatenate([qr_ref[0, :, hh * LANES:hh * LANES + DR_B] for hh in heads], axis=0)
        o_lat = attend((qlat, qrope), PATTN_STACK,
                       lambda q: _dot_t(q[0], ckvb) + _dot_t(q[1], krb), cbias, ckvb)
        for j, hh in enumerate(heads):
            o = _dot(o_lat[j * tq:(j + 1) * tq].astype(BF16), wuv_ref[hh])
            o_ref[0, :, base + hh * DV_B:base + (hh + 1) * DV_B] = o.astype(o_ref.dtype)


def _pattn(qa, iq, iw, ql, qr, ka, va, ik, ckv, kr, wuvt, *, tq):
    b, s_len, _ = qa.shape
    topk = min(TOPK_MAX, s_len // 4)
    mix = H_A * DH_A + H_B * DV_B
    n_seg = max(1, min(4, s_len // tq))
    seg = s_len // n_seg
    nq_seg = seg // tq
    outs = []
    for v in range(n_seg):
        nk = (v + 1) * seg
        q_off = v * nq_seg
        qspec = lambda w, q_off=q_off: pl.BlockSpec((1, tq, w), lambda bi, qi: (bi, q_off + qi, 0))
        kspec = lambda w, nk=nk: pl.BlockSpec((1, nk, w), lambda bi, qi: (bi, 0, 0))
        outs.append(pl.pallas_call(
            functools.partial(_pattn_kernel, topk=topk, q_off=q_off),
            grid=(b, nq_seg),
            in_specs=[qspec(H_A * DH_A), qspec(H_IDX * LANES), qspec(H_IDX), qspec(H_B * KV_LORA),
                      qspec(H_B * LANES), kspec(KV_A * DH_A), kspec(KV_A * DH_A), kspec(D_IDX),
                      kspec(KV_LORA), kspec(DR_B),
                      pl.BlockSpec((H_B, KV_LORA, DV_B), lambda bi, qi: (0, 0, 0))],
            out_specs=pl.BlockSpec((1, tq, mix), lambda bi, qi: (bi, qi, 0)),
            out_shape=jax.ShapeDtypeStruct((b, seg, mix), BF16),
            scratch_shapes=[pltpu.VMEM((tq, nk), F32), pltpu.VMEM((tq, nk), F32)],
            compiler_params=_cparams(("arbitrary", "arbitrary")),
            name=f"pattn{v}",
        )(qa, iq, iw, ql, qr, ka, va, ik, ckv, kr, wuvt))
    return jnp.concatenate(outs, axis=1)


def _stream_chunks(pt_ref, srcs, bufs, lane_paged, sems, body_fn, init, *, layer, pages_per_chunk,
                   n_chunks, entries):
    b, nb = pl.program_id(0), pl.num_programs(0)
    g0 = b * n_chunks

    def copies(step, chunk, slot):
        out = []
        for e in range(entries):
            for p in range(pages_per_chunk):
                page = pt_ref[step * entries + e, chunk * pages_per_chunk + p]
                for k, (src, buf) in enumerate(zip(srcs, bufs)):
                    r, c = src.shape[2], src.shape[3]
                    dst = (buf.at[slot, e, :, pl.ds(p * c, c)] if lane_paged[k]
                           else buf.at[slot, e, pl.ds(p * r, r)])
                    out.append(pltpu.make_async_copy(src.at[layer, page], dst, sems.at[k, slot]))
        return out

    def start(step, chunk, slot):
        for cp in copies(step, chunk, slot):
            cp.start()

    @pl.when(b == 0)
    def _():
        start(b, 0, 0)

    def body(c, carry):
        slot = (g0 + c) % 2

        @pl.when(c + 1 < n_chunks)
        def _():
            start(b, c + 1, 1 - slot)

        @pl.when((c + 1 == n_chunks) & (b + 1 < nb))
        def _():
            start(b + 1, 0, 1 - slot)

        for cp in copies(b, c, slot):
            cp.wait()
        return body_fn(c, slot, carry)

    return lax.fori_loop(0, n_chunks, body, init)


def _score_shape(nb, t, n_past, pages_per_chunk):
    ck = pages_per_chunk * PAGE_SIZE
    return (nb, (n_past // ck + 1) * t, ck)


def _online_update(s, valid, v, m, l, acc):
    if valid is not None:
        s = jnp.where(valid, s, NEG)
    m_new = jnp.maximum(m, jnp.max(s, axis=-1, keepdims=True))
    p = jnp.exp(s - m_new)
    if valid is not None:
        p = jnp.where(valid, p, 0.0)
    alpha = jnp.exp(m - m_new)
    l = alpha * l + jnp.sum(p, axis=-1, keepdims=True)
    acc = alpha * acc + _dot(p.astype(BF16), v)
    return m_new, l, acc


def _stack_heads(ref, n_heads, stride, width):
    rows = [ref[:, hh * stride:hh * stride + width] for hh in range(n_heads)]
    return jnp.concatenate(rows, axis=0).astype(BF16)


def _pad_rows(ref, n):
    t, w = ref.shape
    return jnp.concatenate([ref[...], jnp.zeros((n - t, w), F32)], axis=0).astype(BF16)


def _smla_kernel(pt_ref, iq_ref, iw_ref, ql_ref, qr_ref, ikn_ref, ckvn_ref, krn_ref, wuv_ref,
                 cidx_hbm, cckv_hbm, ckr_hbm, sc_ref, ob_ref, ibuf, cbuf, rbuf, sems,
                 *, t, entries, pages_per_chunk, n_chunks, layer):
    ck = pages_per_chunk * PAGE_SIZE
    rb = H_B * t
    ents = range(entries)
    tok = lambda ref, e: ref.at[e * t:(e + 1) * t]
    iqs = [_stack_heads(tok(iq_ref, e), H_IDX, LANES, D_IDX) for e in ents]
    iw = [tok(iw_ref, e)[...] for e in ents]
    qls = [_stack_heads(tok(ql_ref, e), H_B, KV_LORA, KV_LORA) for e in ents]
    qrs = [_stack_heads(tok(qr_ref, e), H_B, LANES, DR_B) for e in ents]

    def idx_scores(e, dots):
        dd = jnp.maximum(dots, 0.0)
        out = jnp.zeros((t, dots.shape[1]), F32)
        for hh in range(H_IDX):
            out = out + dd[hh * t:(hh + 1) * t] * iw[e][:, hh:hh + 1]
        return out

    def body(c, slot, carry):
        out = []
        for e in ents:
            m, l, acc = carry[e]
            sc_ref[e, pl.ds(pl.multiple_of(c * t, t), t), :] = idx_scores(
                e, _dot(iqs[e], ibuf[slot, e].astype(BF16)))
            ckvb = cbuf[slot, e].astype(BF16)
            s = _dot_t(qls[e], ckvb) + _dot(qrs[e], rbuf[slot, e].astype(BF16))
            out.append(_online_update(s, None, ckvb, m, l, acc))
        return tuple(out)

    init = tuple((jnp.full((rb, 1), NEG, F32), jnp.zeros((rb, 1), F32), jnp.zeros((rb, KV_LORA), F32))
                 for _ in ents)
    carry = _stream_chunks(pt_ref, (cidx_hbm, cckv_hbm, ckr_hbm), (ibuf, cbuf, rbuf),
                           (True, False, True), sems, body, init, layer=layer,
                           pages_per_chunk=pages_per_chunk, n_chunks=n_chunks, entries=entries)

    pad = LANES
    for e in ents:
        m, l, acc = carry[e]
        kj = lax.broadcasted_iota(jnp.int32, (t, pad), 1)
        qi = lax.broadcasted_iota(jnp.int32, (t, pad), 0)
        sn = idx_scores(e, _dot_t(iqs[e], _pad_rows(tok(ikn_ref, e), pad)))
        sn = jnp.where(kj <= qi, sn, -jnp.inf)
        sc_ref[e, n_chunks * t:(n_chunks + 1) * t, :] = jnp.concatenate(
            [sn, jnp.full((t, ck - pad), -jnp.inf, F32)], axis=1)
        ckvn = _pad_rows(tok(ckvn_ref, e), pad)
        s = _dot_t(qls[e], ckvn) + _dot_t(qrs[e], _pad_rows(tok(krn_ref, e), pad))
        kj = lax.broadcasted_iota(jnp.int32, (rb, pad), 1)
        qi = lax.broadcasted_iota(jnp.int32, (rb, pad), 0) % t
        m, l, acc = _online_update(s, kj <= qi, ckvn, m, l, acc)
        o_lat = acc / l
        for hh in range(H_B):
            ob_ref[e * t:(e + 1) * t, hh * DV_B:(hh + 1) * DV_B] = _dot(
                o_lat[hh * t:(hh + 1) * t].astype(BF16), wuv_ref[hh])


def _smla(page_table, iq, iw, ql, qr, ikn, ckvn, krn, wuvt, cache_idx_k, cache_ckv, cache_krope,
          *, t, layer, pages_per_chunk, entries):
    nb, n_pages = page_table.shape
    n_chunks = n_pages // pages_per_chunk
    ck = pages_per_chunk * PAGE_SIZE
    sc_shape = _score_shape(nb, t, n_pages * PAGE_SIZE, pages_per_chunk)
    tok = lambda w: pl.BlockSpec((entries * t, w), lambda b, pt: (b, 0))
    any_spec = pl.BlockSpec(memory_space=pl.ANY)
    grid_spec = pltpu.PrefetchScalarGridSpec(
        num_scalar_prefetch=1,
        grid=(nb // entries,),
        in_specs=[tok(H_IDX * LANES), tok(H_IDX), tok(H_B * KV_LORA), tok(H_B * LANES),
                  tok(D_IDX), tok(KV_LORA), tok(DR_B),
                  pl.BlockSpec((H_B, KV_LORA, DV_B), lambda b, pt: (0, 0, 0)),
                  any_spec, any_spec, any_spec],
        out_specs=[pl.BlockSpec((entries,) + sc_shape[1:], lambda b, pt: (b, 0, 0)),
                   pl.BlockSpec((entries * t, H_B * DV_B), lambda b, pt: (b, 0))],
        scratch_shapes=[pltpu.VMEM((2, entries, D_IDX, ck), F32),
                        pltpu.VMEM((2, entries, ck, KV_LORA), F32),
                        pltpu.VMEM((2, entries, DR_B, ck), F32), pltpu.SemaphoreType.DMA((3, 2))],
    )
    return pl.pallas_call(
        functools.partial(_smla_kernel, t=t, entries=entries, pages_per_chunk=pages_per_chunk,
                          n_chunks=n_chunks, layer=layer),
        grid_spec=grid_spec,
        out_shape=[jax.ShapeDtypeStruct(sc_shape, F32),
                   jax.ShapeDtypeStruct((nb * t, H_B * DV_B), F32)],
        compiler_params=_cparams(("arbitrary",)),
        name="smla",
    )(page_table, iq, iw, ql, qr, ikn, ckvn, krn, wuvt, cache_idx_k, cache_ckv, cache_krope)


def _ssel_kernel(sc_ref, bias_ref, *, topk, groups):
    n = sc_ref.shape[0]
    _select_bias([sc_ref.at[e] for e in range(n)], [bias_ref.at[e] for e in range(n)], topk, groups)


def _ssel(scores, *, t, n_keys, per_step):
    nb, rows, ck = scores.shape
    spec = pl.BlockSpec((per_step, rows, ck), lambda i: (i, 0, 0))
    return pl.pallas_call(
        functools.partial(_ssel_kernel, topk=min(TOPK_MAX, n_keys // 4), groups=rows // t),
        grid=(nb // per_step,),
        in_specs=[spec],
        out_specs=spec,
        out_shape=jax.ShapeDtypeStruct(scores.shape, F32),
        compiler_params=_cparams(("arbitrary",)),
        name="ssel",
    )(scores)


def _sdsa_kernel(pt_ref, bias_ref, qa_ref, kan_ref, van_ref, ck_hbm, cv_hbm, oa_ref,
                 kbuf, vbuf, sems, *, t, entries, pages_per_chunk, n_chunks, layer):
    ck = pages_per_chunk * PAGE_SIZE
    gsz = H_A // KV_A
    rg = gsz * t
    ents = range(entries)
    tok = lambda ref, e: ref.at[e * t:(e + 1) * t]
    qs = [_stack_heads(tok(qa_ref, e), H_A, DH_A, DH_A) for e in ents]

    def group_update(e, g, keys, vals, bias, carry):
        m, l, acc = carry
        s = _dot_t(qs[e][g * rg:(g + 1) * rg], keys) + jnp.concatenate([bias] * gsz, axis=0)
        return _online_update(s, None, vals, m, l, acc)

    def body(c, slot, carry):
        new = []
        for e in ents:
            bias = bias_ref[e, pl.ds(pl.multiple_of(c * t, t), t), :]
            for g in range(KV_A):
                keys = kbuf[slot, e, pl.ds(g, ck, stride=KV_A), :].astype(BF16)
                vals = vbuf[slot, e, pl.ds(g, ck, stride=KV_A), :].astype(BF16)
                new.append(group_update(e, g, keys, vals, bias, carry[e * KV_A + g]))
        return tuple(new)

    init = tuple((jnp.full((rg, 1), NEG, F32), jnp.zeros((rg, 1), F32), jnp.zeros((rg, DH_A), F32))
                 for _ in range(entries * KV_A))
    carry = _stream_chunks(pt_ref, (ck_hbm, cv_hbm), (kbuf, vbuf), (False, False), sems, body, init,
                           layer=layer, pages_per_chunk=pages_per_chunk, n_chunks=n_chunks,
                           entries=entries)
    pad = LANES
    for e in ents:
        bias_n = bias_ref[e, n_chunks * t:(n_chunks + 1) * t, :pad]
        for g in range(KV_A):
            keys = _pad_rows(tok(kan_ref, e).at[:, g * DH_A:(g + 1) * DH_A], pad)
            vals = _pad_rows(tok(van_ref, e).at[:, g * DH_A:(g + 1) * DH_A], pad)
            m, l, acc = group_update(e, g, keys, vals, bias_n, carry[e * KV_A + g])
            o = acc / l
            for hl in range(gsz):
                hh = g * gsz + hl
                oa_ref[e * t:(e + 1) * t, hh * DH_A:(hh + 1) * DH_A] = o[hl * t:(hl + 1) * t]


def _sdsa(page_table, bias, qa, kan, van, cache_k_a, cache_v_a, *, t, layer, pages_per_chunk, entries):
    nb, n_pages = page_table.shape
    n_chunks = n_pages // pages_per_chunk
    ck = pages_per_chunk * PAGE_SIZE
    tok = lambda w: pl.BlockSpec((entries * t, w), lambda b, pt: (b, 0))
    any_spec = pl.BlockSpec(memory_space=pl.ANY)
    grid_spec = pltpu.PrefetchScalarGridSpec(
        num_scalar_prefetch=1,
        grid=(nb // entries,),
        in_specs=[pl.BlockSpec((entries,) + bias.shape[1:], lambda b, pt: (b, 0, 0)),
                  tok(H_A * DH_A), tok(KV_A * DH_A), tok(KV_A * DH_A), any_spec, any_spec],
        out_specs=tok(H_A * DH_A),
        scratch_shapes=[pltpu.VMEM((2, entries, ck * KV_A, DH_A), F32),
                        pltpu.VMEM((2, entries, ck * KV_A, DH_A), F32),
                        pltpu.SemaphoreType.DMA((2, 2))],
    )
    return pl.pallas_call(
        functools.partial(_sdsa_kernel, t=t, entries=entries, pages_per_chunk=pages_per_chunk,
                          n_chunks=n_chunks, layer=layer),
        grid_spec=grid_spec,
        out_shape=jax.ShapeDtypeStruct((nb * t, H_A * DH_A), F32),
        compiler_params=_cparams(("arbitrary",)),
        name="sdsa",
    )(page_table, bias, qa, kan, van, cache_k_a, cache_v_a)


def _oproj_kernel(x_ref, o_ref, gt_ref, sc_ref, sh_ref, g_ref, w_ref, y_ref, h_ref):
    d = x_ref.shape[-1]
    x = x_ref[...].reshape(-1, d)
    o = o_ref[...].reshape(-1, o_ref.shape[-1]).astype(BF16)
    y = x + gt_ref[...].reshape(-1, d) * _dot(o, w_ref[...])
    y_ref[...] = y.reshape(y_ref.shape)
    h = _rms(y, g_ref[...]) * (1.0 + sc_ref[...].reshape(-1, d)) + sh_ref[...].reshape(-1, d)
    h_ref[...] = h.astype(h_ref.dtype).reshape(h_ref.shape)


def _tok_specs(x, tm, per_token_mod):
    d = x.shape[-1]
    if per_token_mod:
        spec = lambda w: pl.BlockSpec((tm, w), lambda i, *_: (i, 0))
        return x.shape[0] // tm, spec, spec(d), 1
    b, t, _ = x.shape
    tps = t // tm
    spec = lambda w: pl.BlockSpec((1, tm, w), lambda i, *_: (i // tps, i % tps, 0))
    return b * tps, spec, pl.BlockSpec((1, 1, d), lambda i, *_: (i // tps, 0, 0)), tps


def _oproj(x, o, gt, sc, sh, g, w_o, *, tm, per_token_mod):
    n_tiles, spec, m_spec, _ = _tok_specs(x, tm, per_token_mod)
    d = x.shape[-1]
    x_spec = spec(d)
    return pl.pallas_call(
        _oproj_kernel,
        grid=(n_tiles,),
        in_specs=[x_spec, spec(o.shape[-1]), m_spec, m_spec, m_spec,
                  pl.BlockSpec((1, d), lambda i: (0, 0)),
                  pl.BlockSpec(w_o.shape, lambda i: (0, 0))],
        out_specs=[x_spec, x_spec],
        out_shape=[jax.ShapeDtypeStruct(x.shape, F32), jax.ShapeDtypeStruct(x.shape, BF16)],
        compiler_params=_cparams(("arbitrary",)),
        name="oproj",
    )(x, o, gt, sc, sh, g, w_o)


def _ffn_kernel(*refs, per_token_mod, tps, t_dec, final_norm):
    if per_token_mod:
        (x_ref, h_ref, gt_ref, wa_ref, wg_ref, cw_ref, cb_ref, wd_ref, gf_ref,
         sa_ref, sg_ref, y_ref, conv_ref, acc_scr) = refs
    else:
        (x_ref, h_ref, gt_ref, wa_ref, wg_ref, cw_ref, cb_ref, wd_ref, gf_ref,
         y_ref, conv_ref, acc_scr, carry_scr) = refs
    d = x_ref.shape[-1]
    i, j = pl.program_id(0), pl.program_id(1)
    nj = pl.num_programs(1)
    tm = acc_scr.shape[0]

    @pl.when(j == 0)
    def _():
        acc_scr[...] = jnp.zeros_like(acc_scr)

    h = h_ref[...].reshape(-1, d)
    tf = wa_ref.shape[1]
    ts = tf // FFN_SUB
    delta = None
    for sb in range(FFN_SUB):
        cs = slice(sb * ts, (sb + 1) * ts)
        us = (_dot(h, wa_ref[:, cs]), _dot(h, wg_ref[:, cs]))
        cw = cw_ref[:, :, cs]
        cb = cb_ref[:, cs]
        vs = []
        if per_token_mod:
            nb = tm // t_dec
            tpos = lax.broadcasted_iota(jnp.int32, (nb, t_dec, ts), 1)
            for k, st_ref in enumerate((sa_ref, sg_ref)):
                u3 = us[k].reshape(nb, t_dec, ts)
                st = st_ref[:, :, cs]
                um1 = pltpu.roll(jnp.where(tpos >= t_dec - 1, st, u3), 1, 1)
                um2 = pltpu.roll(jnp.where(tpos >= t_dec - 2, st, u3), 2, 1)
                v = cb[k] + cw[0, k] * um2 + cw[1, k] * um1 + cw[2, k] * u3
                vs.append(v.reshape(tm, ts))
                conv_ref[:, :, k, cs] = u3[:, t_dec - (CONV_W - 1):, :]
        else:
            first = (i % tps) == 0
            rows = lax.broadcasted_iota(jnp.int32, (tm, ts), 0)
            for k in range(2):
                u = us[k]
                prev = jnp.where(first, 0.0, carry_scr[j, k, :, cs])
                um1 = jnp.where(rows == 0, prev[1:2], pltpu.roll(u, 1, 0))
                um2 = jnp.where(rows == 0, prev[0:1],
                                jnp.where(rows == 1, prev[1:2], pltpu.roll(u, 2, 0)))
                vs.append(cb[k] + cw[0, k] * um2 + cw[1, k] * um1 + cw[2, k] * u)
                tail = u[tm - (CONV_W - 1):, :]
                carry_scr[j, k, :, cs] = tail
                conv_ref[0, :, k, cs] = tail
        va, vg = vs
        act = (va * jax.nn.sigmoid(va)) * vg
        part = _dot(act.astype(BF16), wd_ref[cs, :])
        delta = part if delta is None else delta + part
    acc_scr[...] += delta

    @pl.when(j == nj - 1)
    def _():
        x = x_ref[...].reshape(-1, d)
        x2 = x + gt_ref[...].reshape(-1, d) * acc_scr[...]
        if final_norm:
            x2 = _rms(x2, gf_ref[...])
        y_ref[...] = x2.reshape(y_ref.shape)


def _ffn(x, h, gt, w_up, conv_w, conv_b, w_down, g_final, state8, *, tm, tf, per_token_mod,
         t_dec, final_norm):
    n_tiles, spec, m_spec, tps = _tok_specs(x, tm, per_token_mod)
    d = x.shape[-1]
    x_spec = spec(d)
    f = w_down.shape[0]
    nj = f // tf
    const = lambda *shape: pl.BlockSpec(shape, lambda i, j: (0,) * len(shape))
    in_specs = [x_spec, x_spec, m_spec,
                pl.BlockSpec((d, tf), lambda i, j: (0, j)),
                pl.BlockSpec((d, tf), lambda i, j: (0, nj + j)),
                pl.BlockSpec((CONV_W, 2, tf), lambda i, j: (0, 0, j)),
                pl.BlockSpec((2, tf), lambda i, j: (0, j)),
                pl.BlockSpec((tf, d), lambda i, j: (j, 0)),
                const(1, d)]
    args = [x, h, gt, w_up, w_up, conv_w, conv_b, w_down, g_final]
    scratch = [pltpu.VMEM((tm, d), F32)]
    if per_token_mod:
        nb_blk = tm // t_dec
        n_entries = x.shape[0] // t_dec
        in_specs += [pl.BlockSpec((nb_blk, t_dec, tf), lambda i, j: (i, 0, j)),
                     pl.BlockSpec((nb_blk, t_dec, tf), lambda i, j: (i, 0, nj + j))]
        args += [state8, state8]
        conv_spec = pl.BlockSpec((nb_blk, CONV_W - 1, 2, tf), lambda i, j: (i, 0, 0, j))
    else:
        n_entries = n_tiles
        conv_spec = pl.BlockSpec((1, CONV_W - 1, 2, tf), lambda i, j: (i, 0, 0, j))
        scratch.append(pltpu.VMEM((nj, 2, CONV_W - 1, tf), F32))
    y, conv = pl.pallas_call(
        functools.partial(_ffn_kernel, per_token_mod=per_token_mod, tps=tps, t_dec=t_dec,
                          final_norm=final_norm),
        grid=(n_tiles, nj),
        in_specs=in_specs,
        out_specs=[x_spec, conv_spec],
        out_shape=[jax.ShapeDtypeStruct(x.shape, F32),
                   jax.ShapeDtypeStruct((n_entries, CONV_W - 1, 2, f), F32)],
        scratch_shapes=scratch,
        compiler_params=_cparams(("arbitrary", "arbitrary")),
        name="ffn",
    )(*args)
    return y, (conv if per_token_mod else conv[tps - 1::tps])


def _rope_tables(pos):
    tabs = []
    for half in (DH_A // 2, D_IDX // 2):
        inv = jnp.power(ROPE_THETA, -jnp.arange(half, dtype=F32) / half)
        ang = pos.astype(F32)[:, None] * inv[None, :]
        cos, sin = jnp.cos(ang), jnp.sin(ang)
        reps = LANES // (2 * half)
        tabs.append(jnp.tile(jnp.concatenate([cos, cos], axis=-1), (1, reps)))
        tabs.append(jnp.tile(jnp.concatenate([-sin, sin], axis=-1), (1, reps)))
    return tabs


def _relayout_w_in(w_in):
    sizes = (H_A * DH_A, KV_A * DH_A, KV_A * DH_A, H_IDX * D_IDX, D_IDX, H_IDX, Q_LORA, KV_LORA, DR_B)
    offs = [0]
    for s in sizes:
        offs.append(offs[-1] + s)
    qa, ka, va, iq, ik, iw, cq, ckv, kr = [w_in[:, offs[k]:offs[k + 1]] for k in range(len(sizes))]
    pad = jnp.zeros((w_in.shape[0], LANES - H_IDX), w_in.dtype)
    return jnp.concatenate([qa, ka, va, iq, cq, ckv, ik, kr, iw, pad], axis=1).astype(BF16)


def kernel(x_prompt, x_sample, cache_k_a, cache_v_a, cache_idx_k, cache_ckv, cache_krope, state_conv,
           page_table, c_prompt, c_sample, w_ada, b_ada, g_attn, w_in, g_q, w_q_b, g_kv, w_uk, w_uv,
           w_o, g_ffn, w_up, conv_w, conv_b, w_down, g_final):
    bp, seq, d = x_prompt.shape
    nb, t_dec, _ = x_sample.shape
    depth = w_ada.shape[0]
    f2 = w_up.shape[2]
    f = f2 // 2
    n_past = page_table.shape[1] * PAGE_SIZE
    n_s = nb * t_dec

    tabs_p = _rope_tables(jnp.arange(seq, dtype=jnp.int32))
    tm_s = min(256, n_s)
    pos_s = n_past + (jnp.arange(tm_s, dtype=jnp.int32) % t_dec)
    tabs_s = _rope_tables(pos_s)

    cidx_t = jnp.swapaxes(cache_idx_k, 2, 3)
    ckr_t = jnp.swapaxes(cache_krope, 2, 3)
    n_pool = cache_k_a.shape[1]
    ck_rows = cache_k_a.reshape(depth, n_pool, PAGE_SIZE * KV_A, DH_A)
    cv_rows = cache_v_a.reshape(depth, n_pool, PAGE_SIZE * KV_A, DH_A)

    xp = x_prompt
    xs = x_sample.reshape(n_s, d)
    new_p, new_s = [], []
    c_rows = bp + nb
    c_pad = (-c_rows) % 8
    c_all = jnp.concatenate([c_prompt, c_sample, jnp.zeros((c_pad, d), F32)], axis=0)
    for l in range(depth):
        mod = _ada(c_all, w_ada[l], b_ada[l])
        mods = jnp.split(mod, 6, axis=-1)
        mp = [m[:bp].reshape(bp, 1, d) for m in mods]
        ms = [jnp.repeat(m[bp:bp + nb], t_dec, axis=0) for m in mods]
        w_in2 = _relayout_w_in(w_in[l])
        wqb = w_q_b[l]
        wqb2 = jnp.concatenate([wqb[:, :, :DN_B].reshape(Q_LORA, H_B * DN_B),
                                wqb[:, :, DN_B:].reshape(Q_LORA, H_B * DR_B)], axis=1).astype(BF16)
        wukt = jnp.transpose(w_uk[l], (1, 2, 0)).astype(BF16)
        wuvt = jnp.transpose(w_uv[l], (1, 0, 2)).astype(BF16)
        w_o_b = w_o[l].astype(BF16)
        w_up_b = w_up[l].astype(BF16)
        w_down_b = w_down[l].astype(BF16)
        cw = conv_w[l].reshape(CONV_W, 2, f)
        cb = conv_b[l].reshape(2, f)
        row = lambda v: v.reshape(1, -1)

        (qa, ka, va, iq, ik, iw, ql, qr, ckv, kr) = _proj(
            xp, mp[1], mp[0], row(g_attn[l]), w_in2, tabs_p, row(g_q[l]), wqb2, wukt, row(g_kv[l]),
            tm=256, per_token_mod=False, qdt=BF16)
        r3 = lambda a: a.reshape(bp, seq, a.shape[-1])
        o = _pattn(r3(qa), r3(iq), r3(iw), r3(ql), r3(qr), r3(ka), r3(va), r3(ik), r3(ckv), r3(kr),
                   wuvt, tq=128)
        last = l == depth - 1
        x1, h2 = _oproj(xp, o, mp[2], mp[4], mp[3], row(g_ffn[l]), w_o_b, tm=512, per_token_mod=False)
        xp, conv_p = _ffn(x1, h2, mp[5], w_up_b, cw, cb, w_down_b, row(g_final), None,
                          tm=512, tf=512, per_token_mod=False, t_dec=t_dec, final_norm=last)
        new_p.append((ka.reshape(bp, seq, KV_A, DH_A), va.reshape(bp, seq, KV_A, DH_A),
                      r3(ik), r3(ckv), r3(kr), conv_p.reshape(bp, CONV_W - 1, f2)))

        (qa, ka, va, iq, ik, iw, ql, qr, ckv, kr) = _proj(
            xs, ms[1], ms[0], row(g_attn[l]), w_in2, tabs_s, row(g_q[l]), wqb2, wukt, row(g_kv[l]),
            tm=tm_s, per_token_mod=True, qdt=F32)
        ppc = min(PAGES_PER_CHUNK, page_table.shape[1])
        ents = SAMPLE_ENTRIES if nb % SAMPLE_ENTRIES == 0 else 1
        scores, ob = _smla(page_table, iq, iw, ql, qr, ik, ckv, kr, wuvt, cidx_t, cache_ckv, ckr_t,
                           t=t_dec, layer=l, pages_per_chunk=ppc, entries=ents)
        bias = _ssel(scores, t=t_dec, n_keys=n_past + t_dec, per_step=4 if nb % 4 == 0 else 1)
        oa = _sdsa(page_table, bias, qa, ka, va, ck_rows, cv_rows, t=t_dec, layer=l,
                   pages_per_chunk=ppc, entries=ents)
        o = jnp.concatenate([oa, ob], axis=-1)
        x1, h2 = _oproj(xs, o, ms[2], ms[4], ms[3], row(g_ffn[l]), w_o_b, tm=min(256, n_s),
                        per_token_mod=True)
        state8 = jnp.pad(state_conv[l], ((0, 0), (t_dec - (CONV_W - 1), 0), (0, 0)))
        xs, conv_s = _ffn(x1, h2, ms[5], w_up_b, cw, cb, w_down_b, row(g_final), state8,
                          tm=min(512, n_s), tf=512, per_token_mod=True, t_dec=t_dec, final_norm=last)
        r3s = lambda a: a.reshape(nb, t_dec, a.shape[-1])
        new_s.append((ka.reshape(nb, t_dec, KV_A, DH_A), va.reshape(nb, t_dec, KV_A, DH_A),
                      r3s(ik), r3s(ckv), r3s(kr), conv_s.reshape(nb, CONV_W - 1, f2)))

    def stack(groups, k):
        return jnp.stack([grp[k] for grp in groups], axis=0)

    return (xp, xs.reshape(nb, t_dec, d),
            stack(new_p, 0), stack(new_p, 1), stack(new_p, 2), stack(new_p, 3), stack(new_p, 4),
            stack(new_p, 5),
            stack(new_s, 0), stack(new_s, 1), stack(new_s, 2), stack(new_s, 3), stack(new_s, 4),
            stack(new_s, 5))
```
